```python
import math
import jax, jax.numpy as jnp
from jax import lax
import numpy as np

D_MODEL = 4096
BATCH = 1
SEQ = 16384
DEPTH = 1

EPS = 1e-6
BIG = 1e30
SSM_HEADS = 32
SSM_HEAD_DIM = 64
SSM_INNER = SSM_HEADS * SSM_HEAD_DIM
SSM_GROUPS = 8
SSM_STATE = 128
SSM_CONV = 4
SSM_CHUNK = 128
SSM_XBC = SSM_INNER + 2 * SSM_GROUPS * SSM_STATE
NSA_HEADS = 16
NSA_KV_GROUPS = 2
NSA_HEAD_DIM = 128
NSA_INNER = NSA_HEADS * NSA_HEAD_DIM
KV_W = NSA_KV_GROUPS * NSA_HEAD_DIM
CMP_STRIDE = 16
CMP_LEN = 2 * CMP_STRIDE
CMP_HIDDEN = 256
SEL_LEN = 64
SEL_TOP = 16
WINDOW = 512
Q_BLOCK = 128
ROPE_THETA = 500000.0
ROPE_DIM = NSA_HEAD_DIM // 4
IN_SPLITS = (SSM_INNER,
             SSM_XBC,
             SSM_HEADS,
             NSA_INNER,
             KV_W, KV_W,
             KV_W, KV_W,
             KV_W, KV_W,
             NSA_HEADS * 3,
             NSA_INNER,
             D_MODEL, D_MODEL)
D_IN = int(sum(IN_SPLITS))
IN_OFFSETS = tuple(int(v) for v in np.cumsum(IN_SPLITS)[:-1])

kernel_name = "hybrid_ssd_nsa_gated_block"


def rmsnorm(x, w):
    xf = x.astype(jnp.float32)
    y = xf * lax.rsqrt(jnp.mean(xf * xf, axis=-1, keepdims=True) + EPS)
    return (y * w.astype(jnp.float32)).astype(x.dtype)


def masked_softmax(s, mask):
    s = jnp.where(mask, s.astype(jnp.float32), -BIG)
    m = jnp.max(s, axis=-1, keepdims=True)
    e = jnp.where(mask, jnp.exp(s - m), 0.0)
    return e / jnp.maximum(jnp.sum(e, axis=-1, keepdims=True), 1e-30)


def partial_rope(t, cos, sin):
    half = ROPE_DIM // 2
    cos = cos[None, :, None, :].astype(t.dtype)
    sin = sin[None, :, None, :].astype(t.dtype)
    x1, x2 = t[..., :half], t[..., half:ROPE_DIM]
    return jnp.concatenate([x1 * cos - x2 * sin, x2 * cos + x1 * sin, t[..., ROPE_DIM:]], axis=-1)


def causal_depthwise_conv(x, w, b):
    y = lax.conv_general_dilated(x, w[:, None, :].astype(x.dtype), window_strides=(1,),
                                 padding=[(SSM_CONV - 1, 0)],
                                 dimension_numbers=("NWC", "WIO", "NWC"),
                                 feature_group_count=x.shape[-1])
    return y + b


def ssd_chunked(xs, dt, a, bm, cm):
    B_, S_, H, P = xs.shape
    G, N = bm.shape[2], bm.shape[3]
    R = H // G
    L = SSM_CHUNK
    nc = S_ // L
    xdt = (xs * dt[..., None]).reshape(B_, nc, L, G, R, P)
    adt = (dt * a).reshape(B_, nc, L, G, R).transpose(0, 3, 4, 1, 2)
    bc = bm.reshape(B_, nc, L, G, N)
    cc = cm.reshape(B_, nc, L, G, N)
    a_cs = jnp.cumsum(adt, axis=-1)
    causal = jnp.tril(jnp.ones((L, L), dtype=bool))
    seg = a_cs[..., :, None] - a_cs[..., None, :]
    decay_in = jnp.exp(jnp.where(causal, seg, -jnp.inf))
    cb = jnp.einsum("bclgn,bcsgn->bcgls", cc, bc)
    y_diag = jnp.einsum("bcgls,bgrcls,bcsgrp->bclgrp", cb, decay_in, xdt)
    decay_to_end = jnp.exp(a_cs[..., -1:] - a_cs)
    states = jnp.einsum("bcsgn,bgrcs,bcsgrp->bcgrpn", bc, decay_to_end, xdt)
    chunk_decay = jnp.exp(a_cs[..., -1])

    def step(h, inp):
        st, dc = inp
        return h * dc[..., None, None] + st, h

    _, prev = lax.scan(step, jnp.zeros_like(states[:, 0]),
                       (jnp.moveaxis(states, 1, 0), jnp.moveaxis(chunk_decay, 3, 0)))
    prev = jnp.moveaxis(prev, 0, 1)
    y_off = jnp.einsum("bclgn,bcgrpn,bgrcl->bclgrp", cc, prev, jnp.exp(a_cs))
    return (y_diag + y_off).reshape(B_, S_, H, P)


def compress_blocks(kv, pe, w1, w2):
    B_, S_, G, D = kv.shape
    halves = kv.reshape(B_, S_ // CMP_STRIDE, CMP_STRIDE, G, D)
    blocks = jnp.concatenate([halves[:, :-1], halves[:, 1:]], axis=2)
    blocks = blocks + pe[None, None, :, None, :]
    flat = blocks.transpose(0, 1, 3, 2, 4).reshape(B_, -1, G, CMP_LEN * D)
    return jax.nn.gelu(flat @ w1) @ w2


def nsa_attention(q_raw, q_rot, k_c, v_c, k_s, v_s, k_w, v_w, gates):
    B_, S_ = q_raw.shape[:2]
    G, R, D = NSA_KV_GROUPS, NSA_HEADS // NSA_KV_GROUPS, NSA_HEAD_DIM
    nc = k_c.shape[1]
    ns = S_ // SEL_LEN
    k_top = min(SEL_TOP, ns)
    scale = D ** -0.5
    cmp_end = jnp.arange(nc) * CMP_STRIDE + CMP_LEN - 1
    ic = jnp.arange(nc)[:, None]
    jc = jnp.arange(ns)[None, :]
    overlap = ((ic * CMP_STRIDE < (jc + 1) * SEL_LEN) &
               (ic * CMP_STRIDE + CMP_LEN > jc * SEL_LEN)).astype(jnp.float32)
    ks_blocks = k_s.reshape(B_, ns, SEL_LEN, G, D).transpose(0, 3, 1, 2, 4)
    vs_blocks = v_s.reshape(B_, ns, SEL_LEN, G, D).transpose(0, 3, 1, 2, 4)
    pad = ((0, 0), (WINDOW, 0), (0, 0), (0, 0))
    k_w_pad = jnp.pad(k_w, pad)
    v_w_pad = jnp.pad(v_w, pad)
    bi = jnp.arange(B_)[:, None, None, None]
    gi = jnp.arange(G)[None, :, None, None]
    sel_offsets = jnp.arange(SEL_LEN)
    win_offsets = jnp.arange(Q_BLOCK + WINDOW)
    jb = jnp.arange(ns)

    def block(qb):
        t0 = qb * Q_BLOCK
        pos_q = t0 + jnp.arange(Q_BLOCK)
        qc = lax.dynamic_slice_in_dim(q_raw, t0, Q_BLOCK, 1).reshape(B_, Q_BLOCK, G, R, D)
        qr = lax.dynamic_slice_in_dim(q_rot, t0, Q_BLOCK, 1).reshape(B_, Q_BLOCK, G, R, D)
        gt = lax.dynamic_slice_in_dim(gates, t0, Q_BLOCK, 1).reshape(B_, Q_BLOCK, G, R, 3)
        s = jnp.einsum("bqgrd,bkgd->bgrqk", qc, k_c).astype(jnp.float32) * scale
        p_c = masked_softmax(s, cmp_end[None, :] <= pos_q[:, None])
        o_c = jnp.einsum("bgrqk,bkgd->bqgrd", p_c.astype(v_c.dtype), v_c)
        imp = jnp.einsum("bgrqk,kj->bgqj", p_c, overlap)
        cur = pos_q // SEL_LEN
        eligible = jb[None, :] * SEL_LEN <= pos_q[:, None]
        forced = (jb[None, :] == 0) | (jb[None, :] == cur[:, None]) | (jb[None, :] == cur[:, None] - 1)
        score = jnp.where(forced, BIG, jnp.where(eligible, imp, -BIG))
        _, idx = lax.top_k(score, k_top)
        kg = ks_blocks[bi, gi, idx].reshape(B_, G, Q_BLOCK, k_top * SEL_LEN, D)
        vg = vs_blocks[bi, gi, idx].reshape(B_, G, Q_BLOCK, k_top * SEL_LEN, D)
        kpos = (idx[..., None] * SEL_LEN + sel_offsets).reshape(B_, G, Q_BLOCK, k_top * SEL_LEN)
        s = jnp.einsum("bqgrd,bgqjd->bgrqj", qr, kg).astype(jnp.float32) * scale
        p_s = masked_softmax(s, (kpos <= pos_q[None, None, :, None])[:, :, None])
        o_s = jnp.einsum("bgrqj,bgqjd->bqgrd", p_s.astype(vg.dtype), vg)
        kw = lax.dynamic_slice_in_dim(k_w_pad, t0, Q_BLOCK + WINDOW, 1)
        vw = lax.dynamic_slice_in_dim(v_w_pad, t0, Q_BLOCK + WINDOW, 1)
        kpos_w = t0 - WINDOW + win_offsets
        dlt = pos_q[:, None] - kpos_w[None, :]
        mask_w = (dlt >= 0) & (dlt < WINDOW) & (kpos_w[None, :] >= 0)
        s = jnp.einsum("bqgrd,bkgd->bgrqk", qr, kw).astype(jnp.float32) * scale
        p_w = masked_softmax(s, mask_w)
        o_w = jnp.einsum("bgrqk,bkgd->bqgrd", p_w.astype(vw.dtype), vw)
        return gt[..., 0:1] * o_c + gt[..., 1:2] * o_s + gt[..., 2:3] * o_w

    o = lax.map(block, jnp.arange(S_ // Q_BLOCK))
    return jnp.moveaxis(o, 0, 1).reshape(B_, S_, NSA_HEADS * D)


def hybrid_layer(x, c, cos, sin, w_ada, b_ada, norm_w, w_in, conv_w, conv_b, dt_bias, a_log,
                 d_skip, ssm_norm_w, cmp_pe_k, cmp_pe_v, cmp_k_w1, cmp_k_w2, cmp_v_w1, cmp_v_w2,
                 w_proj_a, w_proj_b, w_out):
    B_, S_, _ = x.shape
    mod = jax.nn.silu(c) @ w_ada + b_ada
    shift, scl, gate = jnp.split(mod, 3, axis=-1)
    h = rmsnorm(x, norm_w) * (1.0 + scl[:, None, :]) + shift[:, None, :]
    proj = h @ w_in
    (z_a, xbc, dt_raw, q, kc, vc, ks, vs, kw, vw, nsa_g, z_b, g_a, g_b) = jnp.split(proj, IN_OFFSETS, axis=-1)

    xbc = jax.nn.silu(causal_depthwise_conv(xbc, conv_w, conv_b))
    xs, bm, cm = jnp.split(xbc, [SSM_INNER, SSM_INNER + SSM_GROUPS * SSM_STATE], axis=-1)
    xs = xs.reshape(B_, S_, SSM_HEADS, SSM_HEAD_DIM)
    bm = bm.reshape(B_, S_, SSM_GROUPS, SSM_STATE)
    cm = cm.reshape(B_, S_, SSM_GROUPS, SSM_STATE)
    dt = jax.nn.softplus(dt_raw.astype(jnp.float32) + dt_bias.astype(jnp.float32))
    a = -jnp.exp(a_log.astype(jnp.float32))
    y = ssd_chunked(xs, dt, a, bm, cm) + d_skip[:, None] * xs
    y_a = rmsnorm(y.reshape(B_, S_, SSM_INNER) * jax.nn.silu(z_a), ssm_norm_w)
    u_a = y_a @ w_proj_a

    hd = NSA_HEAD_DIM
    q = q.reshape(B_, S_, NSA_HEADS, hd)
    q_rot = partial_rope(q, cos, sin)
    kv_shape = (B_, S_, NSA_KV_GROUPS, hd)
    k_c = compress_blocks(kc.reshape(kv_shape), cmp_pe_k, cmp_k_w1, cmp_k_w2)
    v_c = compress_blocks(vc.reshape(kv_shape), cmp_pe_v, cmp_v_w1, cmp_v_w2)
    k_s = partial_rope(ks.reshape(kv_shape), cos, sin)
    k_w = partial_rope(kw.reshape(kv_shape), cos, sin)
    gates = jax.nn.sigmoid(nsa_g).reshape(B_, S_, NSA_HEADS, 3)
    o = nsa_attention(q, q_rot, k_c, v_c, k_s, vs.reshape(kv_shape), k_w, vw.reshape(kv_shape), gates)
    u_b = (o * jax.nn.silu(z_b)) @ w_proj_b

    merged = jax.nn.sigmoid(g_a) * u_a + jax.nn.sigmoid(g_b) * u_b
    out = merged @ w_out
    return x + gate[:, None, :] * out


def setup_inputs(seed: int = 0) -> dict:
    key = jax.random.key(seed)
    k = jax.random.split(key, 24)
    f32 = jnp.float32

    def nrm(kk, shape, s):
        return jax.random.normal(kk, shape, f32) * s

    dt0 = jnp.exp(jax.random.uniform(k[8], (DEPTH, SSM_HEADS), f32, math.log(1e-3), math.log(1e-1)))
    flat_cmp = CMP_LEN * NSA_HEAD_DIM
    return {
        "x": nrm(k[0], (BATCH, SEQ, D_MODEL), 1.0),
        "c": nrm(k[1], (BATCH, D_MODEL), 1.0),
        "w_ada": nrm(k[2], (DEPTH, D_MODEL, 3 * D_MODEL), 0.5 * D_MODEL ** -0.5),
        "b_ada": nrm(k[3], (DEPTH, 3 * D_MODEL), 0.01),
        "norm_w": 1.0 + nrm(k[4], (DEPTH, D_MODEL), 0.02),
        "w_in": nrm(k[5], (DEPTH, D_MODEL, D_IN), D_MODEL ** -0.5),
        "conv_w": nrm(k[6], (DEPTH, SSM_CONV, SSM_XBC), SSM_CONV ** -0.5),
        "conv_b": nrm(k[7], (DEPTH, SSM_XBC), 0.01),
        "dt_bias": dt0 + jnp.log(-jnp.expm1(-dt0)),
        "a_log": jnp.log(jax.random.uniform(k[9], (DEPTH, SSM_HEADS), f32, 1.0, 16.0)),
        "d_skip": 1.0 + nrm(k[10], (DEPTH, SSM_HEADS), 0.02),
        "ssm_norm_w": 1.0 + nrm(k[11], (DEPTH, SSM_INNER), 0.02),
        "cmp_pe_k": nrm(k[12], (DEPTH, CMP_LEN, NSA_HEAD_DIM), 0.02),
        "cmp_pe_v": nrm(k[13], (DEPTH, CMP_LEN, NSA_HEAD_DIM), 0.02),
        "cmp_k_w1": nrm(k[14], (DEPTH, flat_cmp, CMP_HIDDEN), flat_cmp ** -0.5),
        "cmp_k_w2": nrm(k[15], (DEPTH, CMP_HIDDEN, NSA_HEAD_DIM), CMP_HIDDEN ** -0.5),
        "cmp_v_w1": nrm(k[16], (DEPTH, flat_cmp, CMP_HIDDEN), flat_cmp ** -0.5),
        "cmp_v_w2": nrm(k[17], (DEPTH, CMP_HIDDEN, NSA_HEAD_DIM), CMP_HIDDEN ** -0.5),
        "w_proj_a": nrm(k[18], (DEPTH, SSM_INNER, D_MODEL), SSM_INNER ** -0.5),
        "w_proj_b": nrm(k[19], (DEPTH, NSA_INNER, D_MODEL), NSA_INNER ** -0.5),
        "w_out": nrm(k[20], (DEPTH, D_MODEL, D_MODEL), D_MODEL ** -0.5),
        "final_norm_w": 1.0 + nrm(k[21], (D_MODEL,), 0.02),
    }


def reference(x, c, w_ada, b_ada, norm_w, w_in, conv_w, conv_b, dt_bias, a_log, d_skip,
              ssm_norm_w, cmp_pe_k, cmp_pe_v, cmp_k_w1, cmp_k_w2, cmp_v_w1, cmp_v_w2,
              w_proj_a, w_proj_b, w_out, final_norm_w):
    S_ = x.shape[1]
    pos = jnp.arange(S_, dtype=jnp.float32)
    inv_freq = ROPE_THETA ** (-jnp.arange(0, ROPE_DIM, 2, dtype=jnp.float32) / ROPE_DIM)
    ang = pos[:, None] * inv_freq[None, :]
    cos, sin = jnp.cos(ang), jnp.sin(ang)
    for layer in range(DEPTH):
        x = hybrid_layer(x, c, cos, sin, w_ada[layer], b_ada[layer], norm_w[layer], w_in[layer],
                         conv_w[layer], conv_b[layer], dt_bias[layer], a_log[layer], d_skip[layer],
                         ssm_norm_w[layer], cmp_pe_k[layer], cmp_pe_v[layer], cmp_k_w1[layer],
                         cmp_k_w2[layer], cmp_v_w1[layer], cmp_v_w2[layer], w_proj_a[layer],
                         w_proj_b[layer], w_out[layer])
    return rmsnorm(x, final_norm_w)
```

```python
import functools

import numpy as np
import jax
import jax.numpy as jnp
from jax import lax
from jax.experimental import pallas as pl
from jax.experimental.pallas import tpu as pltpu

F32 = jnp.float32
BF16 = jnp.bfloat16

EPS = 1e-6
BIG = 1e30
NEG = -1e30
REMOVED = -3e38

SSM_HEADS = 32
SSM_HEAD_DIM = 64
SSM_INNER = SSM_HEADS * SSM_HEAD_DIM
SSM_GROUPS = 8
SSM_STATE = 128
SSM_CONV = 4
SSM_CHUNK = 128
SSM_GN = SSM_GROUPS * SSM_STATE
SSM_XBC = SSM_INNER + 2 * SSM_GN
SSM_RP = SSM_INNER // SSM_GROUPS

NSA_HEADS = 16
NSA_KV_GROUPS = 2
NSA_R = NSA_HEADS // NSA_KV_GROUPS
NSA_HEAD_DIM = 128
NSA_INNER = NSA_HEADS * NSA_HEAD_DIM
KV_W = NSA_KV_GROUPS * NSA_HEAD_DIM
CMP_STRIDE = 16
CMP_LEN = 32
CMP_HIDDEN = 256
SEL_LEN = 64
SEL_TOP = 16
WINDOW = 512
Q_BLOCK = 128
ROPE_THETA = 500000.0
ROPE_DIM = NSA_HEAD_DIM // 4
ROPE_HALF = ROPE_DIM // 2

LANES = 128
SEL_SLOTS = 128
SEL_KEY_TILE = 512
VMEM_LIMIT = 56 * 1024 * 1024


def _params(sem, vmem=VMEM_LIMIT):
    return pltpu.CompilerParams(dimension_semantics=sem, vmem_limit_bytes=vmem)


def _tile(n, pref=512):
    t = min(pref, n)
    while n % t:
        t //= 2
    return t


def _col_block(layout, name, width):
    assert layout[name] % width == 0, (name, layout[name], width)
    return layout[name] // width


def _silu(v):
    return v * jax.nn.sigmoid(v)


def _softplus(v):
    return jnp.maximum(v, 0.0) + jnp.log1p(jnp.exp(-jnp.abs(v)))


def _split3(v):
    hi = v.astype(BF16)
    r = v - hi.astype(F32)
    mid = r.astype(BF16)
    lo = (r - mid.astype(F32)).astype(BF16)
    return hi, mid, lo


def _dot(a, b):
    return jnp.dot(a, b, preferred_element_type=F32)


def _dot_nt(a, b):
    return lax.dot_general(a, b, (((1,), (1,)), ((), ())), preferred_element_type=F32)


def _dot3_left(m01, v):
    hi, mid, lo = _split3(v)
    return _dot(m01, hi) + _dot(m01, mid) + _dot(m01, lo)


def _dot3_right(v, m01):
    hi, mid, lo = _split3(v)
    return _dot(hi, m01) + _dot(mid, m01) + _dot(lo, m01)


def _proj_layout(d_model):
    segs = [("xbc", SSM_XBC), ("g_a", d_model), ("g_b", d_model), ("z_a", SSM_INNER), ("q", NSA_INNER),
            ("z_b", NSA_INNER), ("kv", 6 * KV_W), ("nsa", NSA_KV_GROUPS * LANES), ("dt", LANES)]
    segs = sorted(segs, key=lambda s: -s[1])
    off, out = 0, {}
    for name, w in segs:
        out[name] = off
        off += w
    return out, off


def _src_offsets(d_model):
    splits = (SSM_INNER, SSM_XBC, SSM_HEADS, NSA_INNER, KV_W, KV_W, KV_W, KV_W, KV_W, KV_W,
              NSA_HEADS * 3, NSA_INNER, d_model, d_model)
    names = ("z_a", "xbc", "dt", "q", "kc", "vc", "ks", "vs", "kw", "vw", "nsa", "z_b", "g_a", "g_b")
    offs = np.concatenate([[0], np.cumsum(splits)])
    return {n: (int(offs[i]), int(splits[i])) for i, n in enumerate(names)}


def _build_w_in(w_in, d_model, n_pad):
    src = _src_offsets(d_model)
    layout, width = _proj_layout(d_model)

    def cols(name, w=None):
        a, n = src[name]
        return w_in[:, a:a + (n if w is None else w)]

    nsa = cols("nsa").reshape(d_model, NSA_KV_GROUPS, NSA_R, 3)
    nsa = jnp.transpose(nsa, (0, 1, 3, 2)).reshape(d_model, NSA_KV_GROUPS, 3 * NSA_R)
    nsa = jnp.pad(nsa, ((0, 0), (0, 0), (0, LANES - 3 * NSA_R))).reshape(d_model, NSA_KV_GROUPS * LANES)
    pieces = {
        "xbc": cols("xbc"), "g_a": cols("g_a"), "g_b": cols("g_b"), "z_a": cols("z_a"), "q": cols("q"),
        "z_b": cols("z_b"), "kv": cols("kc", 6 * KV_W), "nsa": nsa,
        "dt": jnp.pad(cols("dt"), ((0, 0), (0, LANES - SSM_HEADS))),
    }
    order = sorted(layout, key=lambda n: layout[n])
    parts = [pieces[n].astype(BF16) for n in order]
    if n_pad > width:
        parts.append(jnp.zeros((d_model, n_pad - width), BF16))
    return jnp.concatenate(parts, axis=1)


def _mod_kernel(c_ref, w_ref, b_ref, o_ref):
    sc = _silu(c_ref[...])
    o_ref[...] = jnp.sum(w_ref[...] * sc, axis=0, keepdims=True) + b_ref[...]


def _ada_mod(c_col, w_ada, b_ada):
    d, n = w_ada.shape
    tn = _tile(n)
    return pl.pallas_call(
        _mod_kernel,
        grid=(n // tn,),
        in_specs=[pl.BlockSpec((d, 1), lambda j: (0, 0)),
                  pl.BlockSpec((d, tn), lambda j: (0, j)),
                  pl.BlockSpec((1, tn), lambda j: (0, j))],
        out_specs=pl.BlockSpec((1, tn), lambda j: (0, j)),
        out_shape=jax.ShapeDtypeStruct((1, n), F32),
        compiler_params=_params(("arbitrary",)),
        name="ada_mod",
    )(c_col, w_ada, b_ada)


def _in_proj_kernel(x_ref, nw_ref, shift_ref, scl_ref, w_ref, o_ref, h_ref):
    @pl.when(pl.program_id(1) == 0)
    def _():
        x = x_ref[...]
        y = x * lax.rsqrt(jnp.mean(x * x, axis=-1, keepdims=True) + EPS) * nw_ref[...]
        h_ref[...] = (y * (1.0 + scl_ref[...]) + shift_ref[...]).astype(BF16)

    o_ref[...] = _dot(h_ref[...], w_ref[...])


def _in_proj(x2, norm_w, mod, w_in_p):
    s, d = x2.shape
    n = w_in_p.shape[1]
    tm, tn = min(512, s), min(512, n)
    return pl.pallas_call(
        _in_proj_kernel,
        grid=(s // tm, n // tn),
        in_specs=[pl.BlockSpec((tm, d), lambda i, j: (i, 0)),
                  pl.BlockSpec((1, d), lambda i, j: (0, 0)),
                  pl.BlockSpec((1, d), lambda i, j: (0, 0)),
                  pl.BlockSpec((1, d), lambda i, j: (0, 1)),
                  pl.BlockSpec((d, tn), lambda i, j: (0, j))],
        out_specs=pl.BlockSpec((tm, tn), lambda i, j: (i, j)),
        out_shape=jax.ShapeDtypeStruct((s, n), F32),
        scratch_shapes=[pltpu.VMEM((tm, d), BF16)],
        compiler_params=_params(("parallel", "arbitrary")),
        name="in_proj",
    )(x2, norm_w, mod, mod, w_in_p)


def _ssd_kernel(xbc_ref, za_ref, dt_ref, cw_ref, cb_ref, dtb_ref, a_ref, dsk_ref, nw_ref, dtbs_ref, as_ref,
                e_ref, o_ref, xp_ref, st_ref, y_ref):
    L, N, P, RP = SSM_CHUNK, SSM_STATE, SSM_HEAD_DIM, SSM_RP

    @pl.when(pl.program_id(0) == 0)
    def _():
        st_ref[...] = jnp.zeros_like(st_ref)
        xp_ref[pl.ds(L, 8), :] = jnp.zeros((8, SSM_XBC), F32)

    xp_ref[pl.ds(0, 8), :] = xp_ref[pl.ds(L, 8), :]
    x = xbc_ref[...]
    xp_ref[pl.ds(8, L), :] = x
    cw = cw_ref[...]
    conv = (cb_ref[...] + cw[3:4] * x + cw[2:3] * xp_ref[pl.ds(7, L), :]
            + cw[1:2] * xp_ref[pl.ds(6, L), :] + cw[0:1] * xp_ref[pl.ds(5, L), :])
    act = _silu(conv)
    xs = act[:, :SSM_INNER]
    bm = act[:, SSM_INNER:SSM_INNER + SSM_GN]
    cm = act[:, SSM_INNER + SSM_GN:]

    row = lax.broadcasted_iota(jnp.int32, (L, L), 0)
    col = lax.broadcasted_iota(jnp.int32, (L, L), 1)
    causal = row >= col
    tril = causal.astype(BF16)

    dtr = dt_ref[...]
    dt = _softplus(_dot3_right(dtr, e_ref[...]) + dtb_ref[...])
    a_cs = _dot3_left(tril, dt * a_ref[...])
    a_cs_s = _dot3_left(tril, _softplus(dtr + dtbs_ref[...]) * as_ref[...])
    a_cs_t = a_cs_s.T

    xdt = xs * dt
    ea = jnp.exp(a_cs)
    a_last = a_cs[L - 1:L, :]
    cdec = jnp.exp(a_last)
    xdt_b = xdt.astype(BF16)
    xw_b = (xdt * jnp.exp(a_last - a_cs)).astype(BF16)

    for g in range(SSM_GROUPS):
        bg = bm[:, g * N:(g + 1) * N]
        cg_b = cm[:, g * N:(g + 1) * N].astype(BF16)
        cb = _dot_nt(cg_b, bg.astype(BF16))
        st = st_ref[g]
        y_off = _dot(cg_b, st.astype(BF16)) * ea[:, g * RP:(g + 1) * RP]
        for r in range(SSM_HEADS // SSM_GROUPS):
            h = g * (SSM_HEADS // SSM_GROUPS) + r
            seg = a_cs_s[:, h:h + 1] - a_cs_t[h:h + 1, :]
            dec = jnp.exp(jnp.where(causal, seg, NEG))
            y_d = _dot((cb * dec).astype(BF16), xdt_b[:, h * P:(h + 1) * P])
            y_ref[:, h * P:(h + 1) * P] = y_d + y_off[:, r * P:(r + 1) * P]
        new = _dot(bg.T.astype(BF16), xw_b[:, g * RP:(g + 1) * RP])
        st_ref[g] = st * cdec[:, g * RP:(g + 1) * RP] + new

    y = y_ref[...] + dsk_ref[...] * xs
    yg = y * _silu(za_ref[...])
    o_ref[...] = (yg * lax.rsqrt(jnp.mean(yg * yg, axis=-1, keepdims=True) + EPS) * nw_ref[...]).astype(o_ref.dtype)


def _ssd(proj, layout, conv_w, conv_b, dt_bias, a_log, d_skip, ssm_norm_w):
    s = proj.shape[0]
    L = SSM_CHUNK
    rep = lambda v: jnp.repeat(v.astype(F32), SSM_HEAD_DIM)[None, :]
    pad = lambda v: jnp.pad(v.astype(F32), (0, LANES - SSM_HEADS))[None, :]
    a = -jnp.exp(a_log.astype(F32))
    expand = np.zeros((LANES, SSM_INNER), np.float32)
    expand[np.arange(SSM_INNER) // SSM_HEAD_DIM, np.arange(SSM_INNER)] = 1.0
    const = lambda shape: pl.BlockSpec(shape, lambda c: (0,) * len(shape))
    return pl.pallas_call(
        _ssd_kernel,
        grid=(s // L,),
        in_specs=[pl.BlockSpec((L, SSM_XBC), lambda c: (c, _col_block(layout, "xbc", SSM_XBC))),
                  pl.BlockSpec((L, SSM_INNER), lambda c: (c, _col_block(layout, "z_a", SSM_INNER))),
                  pl.BlockSpec((L, LANES), lambda c: (c, _col_block(layout, "dt", LANES))),
                  const((SSM_CONV, SSM_XBC)), const((1, SSM_XBC)),
                  const((1, SSM_INNER)), const((1, SSM_INNER)), const((1, SSM_INNER)), const((1, SSM_INNER)),
                  const((1, LANES)), const((1, LANES)), const((LANES, SSM_INNER))],
        out_specs=pl.BlockSpec((L, SSM_INNER), lambda c: (c, 0)),
        out_shape=jax.ShapeDtypeStruct((s, SSM_INNER), BF16),
        scratch_shapes=[pltpu.VMEM((L + 8, SSM_XBC), F32),
                        pltpu.VMEM((SSM_GROUPS, SSM_STATE, SSM_RP), F32),
                        pltpu.VMEM((L, SSM_INNER), F32)],
        compiler_params=_params(("arbitrary",)),
        name="ssd",
    )(proj, proj, proj, conv_w, conv_b[None, :], rep(dt_bias), rep(a), rep(d_skip), ssm_norm_w[None, :],
      pad(dt_bias), pad(a), jnp.asarray(expand, BF16))


def _prep_kernel(q_ref, ks_ref, vsi_ref, kwi_ref, vwi_ref, cos_ref, sa_ref, sb_ref, qr_ref, qt_ref, ksa_ref, vs_ref,
                 kw_ref, vw_ref, *, tm, slots):
    cosf, sa, sb = cos_ref[...], sa_ref[...], sb_ref[...]
    hd = NSA_HEAD_DIM

    def rope(t):
        return t * cosf + pltpu.roll(t, hd - ROPE_HALF, 1) * sa + pltpu.roll(t, ROPE_HALF, 1) * sb

    scale = hd ** -0.5
    for h in range(NSA_HEADS):
        t = q_ref[:, h * hd:(h + 1) * hd]
        qr_ref[:, h * hd:(h + 1) * hd] = (t * scale).astype(BF16)
        qt_ref[:, h * hd:(h + 1) * hd] = (rope(t) * scale).astype(BF16)

    pos = pl.program_id(0) * tm + lax.broadcasted_iota(jnp.int32, (tm, slots), 0)
    lane = lax.broadcasted_iota(jnp.int32, (tm, slots), 1)
    onehot = jnp.where((pos // SEL_LEN) % slots == lane, 1.0, 0.0).astype(BF16)
    for g in range(NSA_KV_GROUPS):
        cols = slice(g * hd, (g + 1) * hd)
        ksa_ref[g, :, :hd] = rope(ks_ref[:, cols]).astype(BF16)
        ksa_ref[g, :, hd:] = onehot
        vs_ref[g] = vsi_ref[:, cols].astype(BF16)
        kw_ref[g] = rope(kwi_ref[:, cols]).astype(BF16)
        vw_ref[g] = vwi_ref[:, cols].astype(BF16)


def _nsa_prep(proj, layout, cos, sin, slots):
    s = proj.shape[0]
    tm = min(512, s)
    hd, G = NSA_HEAD_DIM, NSA_KV_GROUPS
    ones = jnp.ones((s, hd - ROPE_DIM), F32)
    zeros = lambda w: jnp.zeros((s, w), F32)
    cosf = jnp.concatenate([cos, cos, ones], axis=1)
    sa = jnp.concatenate([-sin, zeros(hd - ROPE_HALF)], axis=1)
    sb = jnp.concatenate([zeros(ROPE_HALF), sin, zeros(hd - ROPE_DIM)], axis=1)
    tab = pl.BlockSpec((tm, hd), lambda i: (i, 0))
    grp = lambda w: pl.BlockSpec((G, tm, w), lambda i: (0, i, 0))
    kvb = _col_block(layout, "kv", KV_W)
    return pl.pallas_call(
        functools.partial(_prep_kernel, tm=tm, slots=slots),
        grid=(s // tm,),
        in_specs=[pl.BlockSpec((tm, NSA_INNER), lambda i: (i, _col_block(layout, "q", NSA_INNER))),
                  *[pl.BlockSpec((tm, KV_W), functools.partial(lambda i, k: (i, k), k=kvb + k)) for k in (2, 3, 4, 5)],
                  tab, tab, tab],
        out_specs=[pl.BlockSpec((tm, NSA_INNER), lambda i: (i, 0)),
                   pl.BlockSpec((tm, NSA_INNER), lambda i: (i, 0)),
                   grp(hd + slots), grp(hd), grp(hd), grp(hd)],
        out_shape=[jax.ShapeDtypeStruct((s, NSA_INNER), BF16), jax.ShapeDtypeStruct((s, NSA_INNER), BF16),
                   jax.ShapeDtypeStruct((G, s, hd + slots), BF16), jax.ShapeDtypeStruct((G, s, hd), BF16),
                   jax.ShapeDtypeStruct((G, s, hd), BF16), jax.ShapeDtypeStruct((G, s, hd), BF16)],
        compiler_params=_params(("parallel",)),
        name="nsa_prep",
    )(proj, proj, proj, proj, proj, cosf, sa, sb)


def _cmp_kernel(x_ref, pe_ref, w1_ref, w2_ref, o_ref):
    x = x_ref[0, 0]
    nc1, half = x.shape
    a = _dot((x + pe_ref[0, 0:1]).astype(BF16), w1_ref[0, :half])
    b = _dot((x + pe_ref[0, 1:2]).astype(BF16), w1_ref[0, half:])
    pre = a + pltpu.roll(b, nc1 - 1, 0)
    hdn = 0.5 * pre * (1.0 + jnp.tanh(0.7978845608028654 * (pre + 0.044715 * pre * pre * pre)))
    o_ref[0, 0] = _dot(hdn.astype(BF16), w2_ref[0]).astype(o_ref.dtype)


def _nsa_cmp(xkv, pe, w1, w2):
    _, G, nc1, half = xkv.shape
    return pl.pallas_call(
        _cmp_kernel,
        grid=(2, G),
        in_specs=[pl.BlockSpec((1, 1, nc1, half), lambda t, g: (t, g, 0, 0)),
                  pl.BlockSpec((1, 2, half), lambda t, g: (t, 0, 0)),
                  pl.BlockSpec((1, 2 * half, CMP_HIDDEN), lambda t, g: (t, 0, 0)),
                  pl.BlockSpec((1, CMP_HIDDEN, NSA_HEAD_DIM), lambda t, g: (t, 0, 0))],
        out_specs=pl.BlockSpec((1, 1, nc1, NSA_HEAD_DIM), lambda t, g: (t, g, 0, 0)),
        out_shape=jax.ShapeDtypeStruct((2, G, nc1, NSA_HEAD_DIM), BF16),
        compiler_params=_params(("parallel", "parallel")),
        name="nsa_cmp",
    )(xkv, pe, w1, w2)


def _stack_heads(q):
    return jnp.concatenate([q[:, r * NSA_HEAD_DIM:(r + 1) * NSA_HEAD_DIM] for r in range(NSA_R)], axis=0)


def _attn_cmp_kernel(q_ref, kc_ref, vc_ref, gt_ref, ov_ref, oc_ref, sb_ref, *, ns, k_top):
    QB, R, hd = Q_BLOCK, NSA_R, NSA_HEAD_DIM
    t0 = pl.program_id(1) * QB
    kc, vc = kc_ref[0, 0], vc_ref[0, 0]
    nc1 = kc.shape[0]
    s3 = _dot_nt(_stack_heads(q_ref[...]), kc).reshape(R, QB, nc1)
    pos = t0 + lax.broadcasted_iota(jnp.int32, (QB, nc1), 0)
    kend = lax.broadcasted_iota(jnp.int32, (QB, nc1), 1) * CMP_STRIDE + (CMP_LEN - 1)
    vis = (kend <= pos)[None]
    sm = jnp.where(vis, s3, NEG)
    e = jnp.where(vis, jnp.exp(sm - jnp.max(sm, axis=-1, keepdims=True)), 0.0)
    p = e / jnp.maximum(jnp.sum(e, axis=-1, keepdims=True), 1e-30)
    oc = _dot(p.reshape(R * QB, nc1).astype(BF16), vc)
    gate = jax.nn.sigmoid(gt_ref[...])
    for r in range(R):
        oc_ref[:, r * hd:(r + 1) * hd] = oc[r * QB:(r + 1) * QB] * gate[:, r:r + 1]

    imp = _dot3_right(jnp.sum(p, axis=0), ov_ref[...])
    jb = lax.broadcasted_iota(jnp.int32, (QB, ns), 1)
    pq = t0 + lax.broadcasted_iota(jnp.int32, (QB, ns), 0)
    cur = pq // SEL_LEN
    eligible = jb * SEL_LEN <= pq
    forced = (jb == 0) | (jb == cur) | (jb == cur - 1)
    score = jnp.where(forced, BIG, jnp.where(eligible, imp, -BIG))
    jbf = jb.astype(F32)

    def take(_, carry):
        score, sel = carry
        mx = jnp.max(score, axis=-1, keepdims=True)
        idx = jnp.min(jnp.where(score == mx, jbf, float(ns)), axis=-1, keepdims=True)
        pick = jbf == idx
        return jnp.where(pick, REMOVED, score), jnp.where(pick, 1.0, sel)

    _, sel = lax.fori_loop(0, k_top, take, (score, jnp.zeros((QB, ns), F32)))
    bias = jnp.where((sel > 0.0) & eligible, 0.0, NEG).astype(BF16)
    nsp = sb_ref.shape[2]
    sb_ref[0, :, :ns] = bias
    if nsp > ns:
        sb_ref[0, :, ns:] = jnp.full((QB, nsp - ns), NEG, BF16)


def _attn_cmp(q_raw, kvc, proj, layout, nsp):
    s = q_raw.shape[0]
    G, QB, hd = NSA_KV_GROUPS, Q_BLOCK, NSA_HEAD_DIM
    nc1 = kvc.shape[2]
    ns = s // SEL_LEN
    ic = np.arange(nc1)[:, None]
    jc = np.arange(ns)[None, :]
    overlap = ((ic * CMP_STRIDE < (jc + 1) * SEL_LEN) & (ic * CMP_STRIDE + CMP_LEN > jc * SEL_LEN)
               & (ic < nc1 - 1)).astype(np.float32)
    gw = NSA_R * hd
    return pl.pallas_call(
        functools.partial(_attn_cmp_kernel, ns=ns, k_top=min(SEL_TOP, ns)),
        grid=(G, s // QB),
        in_specs=[pl.BlockSpec((QB, gw), lambda g, i: (i, g)),
                  pl.BlockSpec((1, 1, nc1, hd), lambda g, i: (0, g, 0, 0)),
                  pl.BlockSpec((1, 1, nc1, hd), lambda g, i: (1, g, 0, 0)),
                  pl.BlockSpec((QB, LANES), lambda g, i: (i, _col_block(layout, "nsa", LANES) + g)),
                  pl.BlockSpec((nc1, ns), lambda g, i: (0, 0))],
        out_specs=[pl.BlockSpec((QB, gw), lambda g, i: (i, g)),
                   pl.BlockSpec((1, QB, nsp), lambda g, i: (g, i, 0))],
        out_shape=[jax.ShapeDtypeStruct((s, NSA_INNER), F32), jax.ShapeDtypeStruct((G, s, nsp), BF16)],
        compiler_params=_params(("parallel", "parallel")),
        name="attn_cmp",
    )(q_raw, kvc, kvc, proj, jnp.asarray(overlap, BF16))


def _attn_sel_kernel(q_ref, sb_ref, ksa_ref, vs_ref, *rest, tk, slots, nwin):
    kw_refs, vw_refs = rest[:nwin], rest[nwin:2 * nwin]
    gt_ref, oc_ref, o_ref, qa_ref = rest[2 * nwin:]
    QB, R, hd = Q_BLOCK, NSA_R, NSA_HEAD_DIM
    RQ = R * QB
    t0 = pl.program_id(1) * QB
    qs = _stack_heads(q_ref[...])
    sb = sb_ref[0]
    for hf in range(qa_ref.shape[0]):
        qa_ref[hf, :, :hd] = qs
        qa_ref[hf, :, hd:] = jnp.concatenate([sb[:, hf * slots:(hf + 1) * slots]] * R, axis=0)
    tiles_per_half = slots * SEL_LEN // tk
    pq = t0 + lax.broadcasted_iota(jnp.int32, (QB, tk), 0)
    kcol = lax.broadcasted_iota(jnp.int32, (QB, tk), 1)

    def tile(kt, carry, diagonal):
        m, l, acc = carry
        k0 = pl.multiple_of(kt * tk, tk)
        s = _dot_nt(qa_ref[kt // tiles_per_half], ksa_ref[0, pl.ds(k0, tk), :])
        if diagonal:
            s = jnp.where((k0 + kcol <= pq)[None], s.reshape(R, QB, tk), NEG).reshape(RQ, tk)
        m_new = jnp.maximum(m, jnp.max(s, axis=-1, keepdims=True))
        alpha = jnp.exp(m - m_new)
        p = jnp.exp(s - m_new)
        l = alpha * l + jnp.sum(p, axis=-1, keepdims=True)
        acc = alpha * acc + _dot(p.astype(BF16), vs_ref[0, pl.ds(k0, tk), :])
        return m_new, l, acc

    kt_last = (t0 + QB - 1) // tk
    init = (jnp.full((RQ, 1), NEG, F32), jnp.zeros((RQ, 1), F32), jnp.zeros((RQ, hd), F32))
    carry = lax.fori_loop(0, kt_last, lambda kt, c: tile(kt, c, False), init)
    _, l_s, acc_s = tile(kt_last, carry, True)

    kw = jnp.concatenate([r[0] for r in kw_refs], axis=0)
    vw = jnp.concatenate([r[0] for r in vw_refs], axis=0)
    nk = nwin * QB
    kpos = t0 - WINDOW + lax.broadcasted_iota(jnp.int32, (QB, nk), 1)
    dlt = t0 + lax.broadcasted_iota(jnp.int32, (QB, nk), 0) - kpos
    vis = ((dlt >= 0) & (dlt < WINDOW) & (kpos >= 0))[None]
    sw = jnp.where(vis, _dot_nt(qs, kw).reshape(R, QB, nk), NEG)
    ew = jnp.where(vis, jnp.exp(sw - jnp.max(sw, axis=-1, keepdims=True)), 0.0)
    pw = ew / jnp.maximum(jnp.sum(ew, axis=-1, keepdims=True), 1e-30)
    ow = _dot(pw.reshape(RQ, nk).astype(BF16), vw)

    gate = jax.nn.sigmoid(gt_ref[...])
    osel = acc_s / jnp.maximum(l_s, 1e-30)
    for r in range(R):
        rows = slice(r * QB, (r + 1) * QB)
        o_ref[:, r * hd:(r + 1) * hd] = (oc_ref[:, r * hd:(r + 1) * hd] + osel[rows] * gate[:, R + r:R + r + 1]
                                        + ow[rows] * gate[:, 2 * R + r:2 * R + r + 1])


def _attn_sel(q_rot, selb, ksa, vs, kw, vw, proj, layout, oc, slots, tk):
    s = q_rot.shape[0]
    G, QB, hd = NSA_KV_GROUPS, Q_BLOCK, NSA_HEAD_DIM
    gw = NSA_R * hd
    nsp = selb.shape[2]
    nwin = WINDOW // QB + 1
    win = [pl.BlockSpec((1, QB, hd), functools.partial(lambda g, i, j: (g, jnp.maximum(i - (nwin - 1) + j, 0), 0), j=j))
           for j in range(nwin)]
    qblk = pl.BlockSpec((QB, gw), lambda g, i: (i, g))
    return pl.pallas_call(
        functools.partial(_attn_sel_kernel, tk=tk, slots=slots, nwin=nwin),
        grid=(G, s // QB),
        in_specs=[qblk,
                  pl.BlockSpec((1, QB, nsp), lambda g, i: (g, i, 0)),
                  pl.BlockSpec((1, s, hd + slots), lambda g, i: (g, 0, 0)),
                  pl.BlockSpec((1, s, hd), lambda g, i: (g, 0, 0)),
                  *win, *win,
                  pl.BlockSpec((QB, LANES), lambda g, i: (i, _col_block(layout, "nsa", LANES) + g)),
                  qblk],
        out_specs=qblk,
        out_shape=jax.ShapeDtypeStruct((s, NSA_INNER), F32),
        scratch_shapes=[pltpu.VMEM((nsp // slots, NSA_R * QB, hd + slots), BF16)],
        compiler_params=_params(("parallel", "arbitrary")),
        name="attn_sel",
    )(q_rot, selb, ksa, vs, *([kw] * nwin), *([vw] * nwin), proj, oc)


def _merge_kernel(ya_ref, o_ref, zb_ref, wa_ref, wb_ref, ga_ref, gb_ref, m_ref, ob_ref):
    @pl.when(pl.program_id(1) == 0)
    def _():
        ob_ref[...] = (o_ref[...] * _silu(zb_ref[...])).astype(BF16)

    u_a = _dot(ya_ref[...], wa_ref[...])
    u_b = _dot(ob_ref[...], wb_ref[...])
    m_ref[...] = (jax.nn.sigmoid(ga_ref[...]) * u_a + jax.nn.sigmoid(gb_ref[...]) * u_b).astype(m_ref.dtype)


def _merge(y_a, o, proj, layout, wa, wb):
    s = y_a.shape[0]
    d = wa.shape[1]
    tm, tn = min(512, s), min(512, d)
    return pl.pallas_call(
        _merge_kernel,
        grid=(s // tm, d // tn),
        in_specs=[pl.BlockSpec((tm, SSM_INNER), lambda i, j: (i, 0)),
                  pl.BlockSpec((tm, NSA_INNER), lambda i, j: (i, 0)),
                  pl.BlockSpec((tm, NSA_INNER), lambda i, j: (i, _col_block(layout, "z_b", NSA_INNER))),
                  pl.BlockSpec((SSM_INNER, tn), lambda i, j: (0, j)),
                  pl.BlockSpec((NSA_INNER, tn), lambda i, j: (0, j)),
                  pl.BlockSpec((tm, tn), lambda i, j: (i, _col_block(layout, "g_a", tn) + j)),
                  pl.BlockSpec((tm, tn), lambda i, j: (i, _col_block(layout, "g_b", tn) + j))],
        out_specs=pl.BlockSpec((tm, tn), lambda i, j: (i, j)),
        out_shape=jax.ShapeDtypeStruct((s, d), BF16),
        scratch_shapes=[pltpu.VMEM((tm, NSA_INNER), BF16)],
        compiler_params=_params(("parallel", "arbitrary")),
        name="merge",
    )(y_a, o, proj, wa, wb, proj, proj)


def _out_kernel(m_ref, w_ref, x_ref, gate_ref, fnw_ref, o_ref, r_ref, *, nt):
    j = pl.program_id(1)
    r_ref[j] = x_ref[...] + gate_ref[...] * _dot(m_ref[...], w_ref[...])

    @pl.when(j == nt - 1)
    def _():
        tn = r_ref.shape[2]
        ss = sum(jnp.sum(r_ref[t] * r_ref[t], axis=-1, keepdims=True) for t in range(nt))
        inv = lax.rsqrt(ss / (nt * tn) + EPS)
        for t in range(nt):
            o_ref[:, t * tn:(t + 1) * tn] = r_ref[t] * inv * fnw_ref[:, t * tn:(t + 1) * tn]


def _out_proj(merged, w_out, x2, mod, fnw):
    s, d = x2.shape
    tm, tn = min(512, s), min(512, d)
    nt = d // tn
    return pl.pallas_call(
        functools.partial(_out_kernel, nt=nt),
        grid=(s // tm, nt),
        in_specs=[pl.BlockSpec((tm, d), lambda i, j: (i, 0)),
                  pl.BlockSpec((d, tn), lambda i, j: (0, j)),
                  pl.BlockSpec((tm, tn), lambda i, j: (i, j)),
                  pl.BlockSpec((1, tn), lambda i, j: (0, 2 * nt + j)),
                  pl.BlockSpec((1, d), lambda i, j: (0, 0))],
        out_specs=pl.BlockSpec((tm, d), lambda i, j: (i, 0)),
        out_shape=jax.ShapeDtypeStruct((s, d), F32),
        scratch_shapes=[pltpu.VMEM((nt, tm, tn), F32)],
        compiler_params=_params(("parallel", "arbitrary")),
        name="out_proj",
    )(merged, w_out, x2, mod, fnw)


def _layer(x2, c, w_ada, b_ada, norm_w, w_in, conv_w, conv_b, dt_bias, a_log, d_skip, ssm_norm_w, cmp_pe_k, cmp_pe_v,
           cmp_k_w1, cmp_k_w2, cmp_v_w1, cmp_v_w2, w_proj_a, w_proj_b, w_out, final_norm_w, cos, sin):
    s, d = x2.shape
    G, hd = NSA_KV_GROUPS, NSA_HEAD_DIM
    layout, width = _proj_layout(d)
    n_pad = -(-width // 512) * 512
    slots = SEL_SLOTS
    nsp = max(s // SEL_LEN, slots)
    assert nsp % slots == 0 and (slots * SEL_LEN) % SEL_KEY_TILE == 0 and s % SEL_KEY_TILE == 0

    mod = _ada_mod(c.reshape(d, 1), w_ada, b_ada[None, :])
    proj = _in_proj(x2, norm_w[None, :], mod, _build_w_in(w_in, d, n_pad))

    y_a = _ssd(proj, layout, conv_w, conv_b, dt_bias, a_log, d_skip, ssm_norm_w)

    q_raw, q_rot, ksa, vs, kw, vw = _nsa_prep(proj, layout, cos, sin, slots)
    kv0 = layout["kv"]
    half_blocks = lambda t: jnp.transpose(t.reshape(s // CMP_STRIDE, CMP_STRIDE, G, hd), (2, 0, 1, 3)).reshape(
        G, s // CMP_STRIDE, CMP_STRIDE * hd)
    xkv = jnp.stack([half_blocks(proj[:, kv0:kv0 + KV_W]), half_blocks(proj[:, kv0 + KV_W:kv0 + 2 * KV_W])])
    pe = jnp.stack([cmp_pe_k.reshape(2, CMP_STRIDE * hd), cmp_pe_v.reshape(2, CMP_STRIDE * hd)])
    kvc = _nsa_cmp(xkv, pe, jnp.stack([cmp_k_w1, cmp_v_w1]).astype(BF16), jnp.stack([cmp_k_w2, cmp_v_w2]).astype(BF16))

    oc, selb = _attn_cmp(q_raw, kvc, proj, layout, nsp)
    o = _attn_sel(q_rot, selb, ksa, vs, kw, vw, proj, layout, oc, slots, SEL_KEY_TILE)

    merged = _merge(y_a, o, proj, layout, w_proj_a.astype(BF16), w_proj_b.astype(BF16))
    return _out_proj(merged, w_out.astype(BF16), x2, mod, final_norm_w[None, :])


def kernel(x, c, w_ada, b_ada, norm_w, w_in, conv_w, conv_b, dt_bias, a_log, d_skip, ssm_norm_w, cmp_pe_k, cmp_pe_v,
           cmp_k_w1, cmp_k_w2, cmp_v_w1, cmp_v_w2, w_proj_a, w_proj_b, w_out, final_norm_w):
    b, s, d = x.shape
    assert b == 1 and w_ada.shape[0] == 1, "one sequence, one layer"
    pos = jnp.arange(s, dtype=F32)
    inv_freq = ROPE_THETA ** (-jnp.arange(0, ROPE_DIM, 2, dtype=F32) / ROPE_DIM)
    ang = pos[:, None] * inv_freq[None, :]
    out = _layer(x[0], c, w_ada[0], b_ada[0], norm_w[0], w_in[0], conv_w[0], conv_b[0], dt_bias[0], a_log[0],
                 d_skip[0], ssm_norm_w[0], cmp_pe_k[0], cmp_pe_v[0], cmp_k_w1[0], cmp_k_w2[0], cmp_v_w1[0],
                 cmp_v_w2[0], w_proj_a[0], w_proj_b[0], w_out[0], final_norm_w, jnp.cos(ang), jnp.sin(ang))
    return out[None]
```

```python
import functools

import numpy as np
import jax
import jax.numpy as jnp
from jax import lax
from jax.experimental import pallas as pl
from jax.experimental.pallas import tpu as pltpu

F32 = jnp.float32
BF16 = jnp.bfloat16

EPS = 1e-6
BIG = 1e30
NEG = -1e30
REMOVED = -3e38

SSM_HEADS = 32
SSM_HEAD_DIM = 64
SSM_INNER = SSM_HEADS * SSM_HEAD_DIM
SSM_GROUPS = 8
SSM_STATE = 128
SSM_CONV = 4
SSM_CHUNK = 128
SSM_GN = SSM_GROUPS * SSM_STATE
SSM_XBC = SSM_INNER + 2 * SSM_GN
SSM_RP = SSM_INNER // SSM_GROUPS

NSA_HEADS = 16
NSA_KV_GROUPS = 2
NSA_R = NSA_HEADS // NSA_KV_GROUPS
NSA_HEAD_DIM = 128
NSA_INNER = NSA_HEADS * NSA_HEAD_DIM
KV_W = NSA_KV_GROUPS * NSA_HEAD_DIM
CMP_STRIDE = 16
CMP_LEN = 32
CMP_HIDDEN = 256
SEL_LEN = 64
SEL_TOP = 16
WINDOW = 512
Q_BLOCK = 128
ROPE_THETA = 500000.0
ROPE_DIM = NSA_HEAD_DIM // 4
ROPE_HALF = ROPE_DIM // 2

LANES = 128
SEL_SLOTS = 128
SEL_KEY_TILE = 512
LOG2E = 1.4426950408889634
VMEM_LIMIT = 56 * 1024 * 1024


def _params(sem, vmem=VMEM_LIMIT):
    return pltpu.CompilerParams(dimension_semantics=sem, vmem_limit_bytes=vmem)


def _tile(n, pref=512):
    t = min(pref, n)
    while n % t:
        t //= 2
    return t


def _col_block(layout, name, width):
    assert layout[name] % width == 0, (name, layout[name], width)
    return layout[name] // width


def _silu(v):
    return v * jax.nn.sigmoid(v)


def _softplus(v):
    return jnp.maximum(v, 0.0) + jnp.log1p(jnp.exp(-jnp.abs(v)))


def _split3(v):
    hi = v.astype(BF16)
    r = v - hi.astype(F32)
    mid = r.astype(BF16)
    lo = (r - mid.astype(F32)).astype(BF16)
    return hi, mid, lo


def _dot(a, b):
    return jnp.dot(a, b, preferred_element_type=F32)


def _dot_nt(a, b):
    return lax.dot_general(a, b, (((1,), (1,)), ((), ())), preferred_element_type=F32)


def _dot3_left(m01, v):
    hi, mid, lo = _split3(v)
    return _dot(m01, hi) + _dot(m01, mid) + _dot(m01, lo)


def _dot3_right(v, m01):
    hi, mid, lo = _split3(v)
    return _dot(hi, m01) + _dot(mid, m01) + _dot(lo, m01)


def _proj_layout(d_model):
    segs = [("xbc", SSM_XBC), ("g_a", d_model), ("g_b", d_model), ("z_a", SSM_INNER), ("q", NSA_INNER),
            ("z_b", NSA_INNER), ("kv", 6 * KV_W), ("nsa", NSA_KV_GROUPS * LANES), ("dt", LANES)]
    segs = sorted(segs, key=lambda s: -s[1])
    off, out = 0, {}
    for name, w in segs:
        out[name] = off
        off += w
    return out, off


def _src_offsets(d_model):
    splits = (SSM_INNER, SSM_XBC, SSM_HEADS, NSA_INNER, KV_W, KV_W, KV_W, KV_W, KV_W, KV_W,
              NSA_HEADS * 3, NSA_INNER, d_model, d_model)
    names = ("z_a", "xbc", "dt", "q", "kc", "vc", "ks", "vs", "kw", "vw", "nsa", "z_b", "g_a", "g_b")
    offs = np.concatenate([[0], np.cumsum(splits)])
    return {n: (int(offs[i]), int(splits[i])) for i, n in enumerate(names)}


def _build_w_in(w_in, d_model, n_pad):
    src = _src_offsets(d_model)
    layout, width = _proj_layout(d_model)

    def cols(name, w=None):
        a, n = src[name]
        return w_in[:, a:a + (n if w is None else w)]

    nsa = cols("nsa").reshape(d_model, NSA_KV_GROUPS, NSA_R, 3)
    nsa = jnp.transpose(nsa, (0, 1, 3, 2)).reshape(d_model, NSA_KV_GROUPS, 3 * NSA_R)
    nsa = jnp.pad(nsa, ((0, 0), (0, 0), (0, LANES - 3 * NSA_R))).reshape(d_model, NSA_KV_GROUPS * LANES)
    pieces = {
        "xbc": cols("xbc"), "g_a": cols("g_a"), "g_b": cols("g_b"), "z_a": cols("z_a"), "q": cols("q"),
        "z_b": cols("z_b"), "kv": cols("kc", 6 * KV_W), "nsa": nsa,
        "dt": jnp.pad(cols("dt"), ((0, 0), (0, LANES - SSM_HEADS))),
    }
    order = sorted(layout, key=lambda n: layout[n])
    parts = [pieces[n].astype(BF16) for n in order]
    if n_pad > width:
        parts.append(jnp.zeros((d_model, n_pad - width), BF16))
    return jnp.concatenate(parts, axis=1)


def _mod_kernel(c_ref, w_ref, b_ref, o_ref):
    sc = _silu(c_ref[...])
    o_ref[...] = jnp.sum(w_ref[...] * sc, axis=0, keepdims=True) + b_ref[...]


def _ada_mod(c_col, w_ada, b_ada):
    d, n = w_ada.shape
    tn = _tile(n)
    return pl.pallas_call(
        _mod_kernel,
        grid=(n // tn,),
        in_specs=[pl.BlockSpec((d, 1), lambda j: (0, 0)),
                  pl.BlockSpec((d, tn), lambda j: (0, j)),
                  pl.BlockSpec((1, tn), lambda j: (0, j))],
        out_specs=pl.BlockSpec((1, tn), lambda j: (0, j)),
        out_shape=jax.ShapeDtypeStruct((1, n), F32),
        compiler_params=_params(("arbitrary",)),
        name="ada_mod",
    )(c_col, w_ada, b_ada)


def _norm_mod_kernel(x_ref, nw_ref, shift_ref, scl_ref, h_ref):
    x = x_ref[...]
    y = x * lax.rsqrt(jnp.mean(x * x, axis=-1, keepdims=True) + EPS) * nw_ref[...]
    h_ref[...] = (y * (1.0 + scl_ref[...]) + shift_ref[...]).astype(BF16)


def _norm_mod(x2, norm_w, mod):
    s, d = x2.shape
    tm = _tile(s, 256)
    return pl.pallas_call(
        _norm_mod_kernel,
        grid=(s // tm,),
        in_specs=[pl.BlockSpec((tm, d), lambda i: (i, 0)),
                  pl.BlockSpec((1, d), lambda i: (0, 0)),
                  pl.BlockSpec((1, d), lambda i: (0, 0)),
                  pl.BlockSpec((1, d), lambda i: (0, 1))],
        out_specs=pl.BlockSpec((tm, d), lambda i: (i, 0)),
        out_shape=jax.ShapeDtypeStruct((s, d), BF16),
        compiler_params=_params(("parallel",)),
        name="norm_mod",
    )(x2, norm_w, mod, mod)


def _in_proj_kernel(h_ref, w_ref, o_ref):
    o_ref[...] = _dot(h_ref[...], w_ref[...])


def _in_proj(h, w_in_p):
    s, d = h.shape
    n = w_in_p.shape[1]
    tm, tn = _tile(s, 1024), _tile(n, 512)
    return pl.pallas_call(
        _in_proj_kernel,
        grid=(s // tm, n // tn),
        in_specs=[pl.BlockSpec((tm, d), lambda i, j: (i, 0)),
                  pl.BlockSpec((d, tn), lambda i, j: (0, j))],
        out_specs=pl.BlockSpec((tm, tn), lambda i, j: (i, j)),
        out_shape=jax.ShapeDtypeStruct((s, n), F32),
        compiler_params=_params(("parallel", "arbitrary")),
        name="in_proj",
    )(h, w_in_p)


def _ssd_kernel(xbc_ref, za_ref, dt_ref, cw_ref, cb_ref, dtb_ref, a_ref, dsk_ref, nw_ref, dtbs_ref, as_ref,
                e_ref, o_ref, xp_ref, st_ref, y_ref):
    L, N, P, RP = SSM_CHUNK, SSM_STATE, SSM_HEAD_DIM, SSM_RP

    @pl.when(pl.program_id(0) == 0)
    def _():
        st_ref[...] = jnp.zeros_like(st_ref)
        xp_ref[pl.ds(L, 8), :] = jnp.zeros((8, SSM_XBC), F32)

    xp_ref[pl.ds(0, 8), :] = xp_ref[pl.ds(L, 8), :]
    x = xbc_ref[...]
    xp_ref[pl.ds(8, L), :] = x
    cw = cw_ref[...]
    conv = (cb_ref[...] + cw[3:4] * x + cw[2:3] * xp_ref[pl.ds(7, L), :]
            + cw[1:2] * xp_ref[pl.ds(6, L), :] + cw[0:1] * xp_ref[pl.ds(5, L), :])
    act = _silu(conv)
    xs = act[:, :SSM_INNER]
    bm = act[:, SSM_INNER:SSM_INNER + SSM_GN]
    cm = act[:, SSM_INNER + SSM_GN:]

    row = lax.broadcasted_iota(jnp.int32, (L, L), 0)
    col = lax.broadcasted_iota(jnp.int32, (L, L), 1)
    causal = row >= col
    tril = causal.astype(BF16)

    dtr = dt_ref[...]
    dt = _softplus(_dot3_right(dtr, e_ref[...]) + dtb_ref[...])
    a_cs = _dot3_left(tril, dt * a_ref[...])
    a_cs_s = _dot3_left(tril, _softplus(dtr + dtbs_ref[...]) * as_ref[...])
    a_cs_t = a_cs_s.T

    xdt = xs * dt
    ea = jnp.exp(a_cs)
    a_last = a_cs[L - 1:L, :]
    cdec = jnp.exp(a_last)
    xdt_b = xdt.astype(BF16)
    xw_b = (xdt * jnp.exp(a_last - a_cs)).astype(BF16)

    for g in range(SSM_GROUPS):
        bg = bm[:, g * N:(g + 1) * N]
        cg_b = cm[:, g * N:(g + 1) * N].astype(BF16)
        cb = _dot_nt(cg_b, bg.astype(BF16))
        st = st_ref[g]
        y_off = _dot(cg_b, st.astype(BF16)) * ea[:, g * RP:(g + 1) * RP]
        for r in range(SSM_HEADS // SSM_GROUPS):
            h = g * (SSM_HEADS // SSM_GROUPS) + r
            seg = a_cs_s[:, h:h + 1] - a_cs_t[h:h + 1, :]
            dec = jnp.exp(jnp.where(causal, seg, NEG))
            y_d = _dot((cb * dec).astype(BF16), xdt_b[:, h * P:(h + 1) * P])
            y_ref[:, h * P:(h + 1) * P] = y_d + y_off[:, r * P:(r + 1) * P]
        new = _dot(bg.T.astype(BF16), xw_b[:, g * RP:(g + 1) * RP])
        st_ref[g] = st * cdec[:, g * RP:(g + 1) * RP] + new

    y = y_ref[...] + dsk_ref[...] * xs
    yg = y * _silu(za_ref[...])
    o_ref[...] = (yg * lax.rsqrt(jnp.mean(yg * yg, axis=-1, keepdims=True) + EPS) * nw_ref[...]).astype(o_ref.dtype)


def _ssd(proj, layout, conv_w, conv_b, dt_bias, a_log, d_skip, ssm_norm_w):
    s = proj.shape[0]
    L = SSM_CHUNK
    rep = lambda v: jnp.repeat(v.astype(F32), SSM_HEAD_DIM)[None, :]
    pad = lambda v: jnp.pad(v.astype(F32), (0, LANES - SSM_HEADS))[None, :]
    a = -jnp.exp(a_log.astype(F32))
    expand = np.zeros((LANES, SSM_INNER), np.float32)
    expand[np.arange(SSM_INNER) // SSM_HEAD_DIM, np.arange(SSM_INNER)] = 1.0
    const = lambda shape: pl.BlockSpec(shape, lambda c: (0,) * len(shape))
    return pl.pallas_call(
        _ssd_kernel,
        grid=(s // L,),
        in_specs=[pl.BlockSpec((L, SSM_XBC), lambda c: (c, _col_block(layout, "xbc", SSM_XBC))),
                  pl.BlockSpec((L, SSM_INNER), lambda c: (c, _col_block(layout, "z_a", SSM_INNER))),
                  pl.BlockSpec((L, LANES), lambda c: (c, _col_block(layout, "dt", LANES))),
                  const((SSM_CONV, SSM_XBC)), const((1, SSM_XBC)),
                  const((1, SSM_INNER)), const((1, SSM_INNER)), const((1, SSM_INNER)), const((1, SSM_INNER)),
                  const((1, LANES)), const((1, LANES)), const((LANES, SSM_INNER))],
        out_specs=pl.BlockSpec((L, SSM_INNER), lambda c: (c, 0)),
        out_shape=jax.ShapeDtypeStruct((s, SSM_INNER), BF16),
        scratch_shapes=[pltpu.VMEM((L + 8, SSM_XBC), F32),
                        pltpu.VMEM((SSM_GROUPS, SSM_STATE, SSM_RP), F32),
                        pltpu.VMEM((L, SSM_INNER), F32)],
        compiler_params=_params(("arbitrary",)),
        name="ssd",
    )(proj, proj, proj, conv_w, conv_b[None, :], rep(dt_bias), rep(a), rep(d_skip), ssm_norm_w[None, :],
      pad(dt_bias), pad(a), jnp.asarray(expand, BF16))


def _prep_kernel(q_ref, ks_ref, vsi_ref, kwi_ref, vwi_ref, cos_ref, sa_ref, sb_ref, qr_ref, qt_ref, ksa_ref, vs_ref,
                 kw_ref, vw_ref, *, tm, slots):
    cosf, sa, sb = cos_ref[...], sa_ref[...], sb_ref[...]
    hd = NSA_HEAD_DIM

    def rope(t):
        return t * cosf + pltpu.roll(t, hd - ROPE_HALF, 1) * sa + pltpu.roll(t, ROPE_HALF, 1) * sb

    scale = hd ** -0.5 * LOG2E
    for h in range(NSA_HEADS):
        t = q_ref[:, h * hd:(h + 1) * hd]
        qr_ref[:, h * hd:(h + 1) * hd] = (t * scale).astype(BF16)
        qt_ref[:, h * hd:(h + 1) * hd] = (rope(t) * scale).astype(BF16)

    pos = pl.program_id(0) * tm + lax.broadcasted_iota(jnp.int32, (tm, slots), 0)
    lane = lax.broadcasted_iota(jnp.int32, (tm, slots), 1)
    onehot = jnp.where((pos // SEL_LEN) % slots == lane, 1.0, 0.0).astype(BF16)
    for g in range(NSA_KV_GROUPS):
        cols = slice(g * hd, (g + 1) * hd)
        ksa_ref[g, :, :hd] = rope(ks_ref[:, cols]).astype(BF16)
        ksa_ref[g, :, hd:] = onehot
        vs_ref[g, :, :hd] = vsi_ref[:, cols].astype(BF16)
        vs_ref[g, :, hd:] = jnp.ones((tm, hd), BF16)
        kw_ref[g] = rope(kwi_ref[:, cols]).astype(BF16)
        vw_ref[g] = vwi_ref[:, cols].astype(BF16)


def _nsa_prep(proj, layout, cos, sin, slots):
    s = proj.shape[0]
    tm = min(512, s)
    hd, G = NSA_HEAD_DIM, NSA_KV_GROUPS
    ones = jnp.ones((s, hd - ROPE_DIM), F32)
    zeros = lambda w: jnp.zeros((s, w), F32)
    cosf = jnp.concatenate([cos, cos, ones], axis=1)
    sa = jnp.concatenate([-sin, zeros(hd - ROPE_HALF)], axis=1)
    sb = jnp.concatenate([zeros(ROPE_HALF), sin, zeros(hd - ROPE_DIM)], axis=1)
    tab = pl.BlockSpec((tm, hd), lambda i: (i, 0))
    grp = lambda w: pl.BlockSpec((G, tm, w), lambda i: (0, i, 0))
    kvb = _col_block(layout, "kv", KV_W)
    return pl.pallas_call(
        functools.partial(_prep_kernel, tm=tm, slots=slots),
        grid=(s // tm,),
        in_specs=[pl.BlockSpec((tm, NSA_INNER), lambda i: (i, _col_block(layout, "q", NSA_INNER))),
                  *[pl.BlockSpec((tm, KV_W), functools.partial(lambda i, k: (i, k), k=kvb + k)) for k in (2, 3, 4, 5)],
                  tab, tab, tab],
        out_specs=[pl.BlockSpec((tm, NSA_INNER), lambda i: (i, 0)),
                   pl.BlockSpec((tm, NSA_INNER), lambda i: (i, 0)),
                   grp(hd + slots), grp(2 * hd), grp(hd), grp(hd)],
        out_shape=[jax.ShapeDtypeStruct((s, NSA_INNER), BF16), jax.ShapeDtypeStruct((s, NSA_INNER), BF16),
                   jax.ShapeDtypeStruct((G, s, hd + slots), BF16), jax.ShapeDtypeStruct((G, s, 2 * hd), BF16),
                   jax.ShapeDtypeStruct((G, s, hd), BF16), jax.ShapeDtypeStruct((G, s, hd), BF16)],
        compiler_params=_params(("parallel",)),
        name="nsa_prep",
    )(proj, proj, proj, proj, proj, cosf, sa, sb)


def _cmp_kernel(x_ref, pe_ref, w1_ref, w2_ref, o_ref):
    x = x_ref[0, 0]
    nc1, half = x.shape
    a = _dot((x + pe_ref[0, 0:1]).astype(BF16), w1_ref[0, :half])
    b = _dot((x + pe_ref[0, 1:2]).astype(BF16), w1_ref[0, half:])
    pre = a + pltpu.roll(b, nc1 - 1, 0)
    hdn = 0.5 * pre * (1.0 + jnp.tanh(0.7978845608028654 * (pre + 0.044715 * pre * pre * pre)))
    o_ref[0, 0] = _dot(hdn.astype(BF16), w2_ref[0]).astype(o_ref.dtype)


def _nsa_cmp(xkv, pe, w1, w2):
    _, G, nc1, half = xkv.shape
    return pl.pallas_call(
        _cmp_kernel,
        grid=(2, G),
        in_specs=[pl.BlockSpec((1, 1, nc1, half), lambda t, g: (t, g, 0, 0)),
                  pl.BlockSpec((1, 2, half), lambda t, g: (t, 0, 0)),
                  pl.BlockSpec((1, 2 * half, CMP_HIDDEN), lambda t, g: (t, 0, 0)),
                  pl.BlockSpec((1, CMP_HIDDEN, NSA_HEAD_DIM), lambda t, g: (t, 0, 0))],
        out_specs=pl.BlockSpec((1, 1, nc1, NSA_HEAD_DIM), lambda t, g: (t, g, 0, 0)),
        out_shape=jax.ShapeDtypeStruct((2, G, nc1, NSA_HEAD_DIM), BF16),
        compiler_params=_params(("parallel", "parallel")),
        name="nsa_cmp",
    )(xkv, pe, w1, w2)


def _stack_heads(q):
    return jnp.concatenate([q[:, r * NSA_HEAD_DIM:(r + 1) * NSA_HEAD_DIM] for r in range(NSA_R)], axis=0)


def _attn_cmp_kernel(q_ref, kc_ref, vc_ref, gt_ref, ov_ref, oc_ref, sb_ref, imp_ref, *, ns, chunk):
    QB, R, hd, G = Q_BLOCK, NSA_R, NSA_HEAD_DIM, NSA_KV_GROUPS
    i = pl.program_id(0)
    t0 = i * QB
    nc1 = kc_ref.shape[2]
    gate = jax.nn.sigmoid(gt_ref[...])

    def branch(nk):
        pos = t0 + lax.broadcasted_iota(jnp.int32, (QB, nk), 0)
        kend = lax.broadcasted_iota(jnp.int32, (QB, nk), 1) * CMP_STRIDE + (CMP_LEN - 1)
        vis = (kend <= pos)[None]
        for g in range(G):
            q = q_ref[:, g * R * hd:(g + 1) * R * hd]
            sm = jnp.where(vis, _dot_nt(_stack_heads(q), kc_ref[0, g, :nk, :]).reshape(R, QB, nk), NEG)
            e = jnp.where(vis, jnp.exp2(sm - jnp.max(sm, axis=-1, keepdims=True)), 0.0)
            p = e / jnp.maximum(jnp.sum(e, axis=-1, keepdims=True), 1e-30)
            oc = _dot(p.reshape(R * QB, nk).astype(BF16), vc_ref[0, g, :nk, :])
            for r in range(R):
                h = g * R + r
                oc_ref[:, h * hd:(h + 1) * hd] = oc[r * QB:(r + 1) * QB] * gate[:, g * LANES + r:g * LANES + r + 1]
            imp_ref[g * QB:(g + 1) * QB, :] = _dot3_right(jnp.sum(p, axis=0), ov_ref[:nk, :])

    need = (i * (QB // CMP_STRIDE) + (QB - CMP_LEN) // CMP_STRIDE + chunk) // chunk
    for n in range(1, nc1 // chunk + 1):
        pl.when(need == n)(functools.partial(branch, n * chunk))

    shape = (G * QB, ns)
    jb = lax.broadcasted_iota(jnp.int32, shape, 1)
    row = lax.broadcasted_iota(jnp.int32, shape, 0)
    pq = t0 + jnp.where(row >= QB, row - QB, row)
    cur = pq // SEL_LEN
    eligible = jb * SEL_LEN <= pq
    forced = (jb == 0) | (jb == cur) | (jb == cur - 1)
    score = jnp.where(forced, REMOVED, jnp.where(eligible, imp_ref[...], -BIG))
    jbf = jb.astype(F32)

    def take(_, carry):
        score, sel = carry
        mx = jnp.max(score, axis=-1, keepdims=True)
        idx = jnp.min(jnp.where(score == mx, jbf, float(ns)), axis=-1, keepdims=True)
        pick = jbf == idx
        return jnp.where(pick, REMOVED, score), jnp.where(pick, 1.0, sel)

    _, sel = lax.fori_loop(0, SEL_TOP - 3, take, (score, jnp.where(forced, 1.0, 0.0)))
    bias = jnp.where((sel > 0.0) & eligible, 0.0, NEG).astype(BF16)
    nsp = sb_ref.shape[2]
    for g in range(G):
        sb_ref[g, :, :ns] = bias[g * QB:(g + 1) * QB]
        if nsp > ns:
            sb_ref[g, :, ns:] = jnp.full((QB, nsp - ns), NEG, BF16)


def _attn_cmp(q_raw, kvc, proj, layout, nsp):
    s = q_raw.shape[0]
    G, QB, hd = NSA_KV_GROUPS, Q_BLOCK, NSA_HEAD_DIM
    nc1 = kvc.shape[2]
    ns = s // SEL_LEN
    ic = np.arange(nc1)[:, None]
    jc = np.arange(ns)[None, :]
    overlap = ((ic * CMP_STRIDE < (jc + 1) * SEL_LEN) & (ic * CMP_STRIDE + CMP_LEN > jc * SEL_LEN)
               & (ic < nc1 - 1)).astype(np.float32)
    assert ns >= SEL_TOP
    chunk = _tile(nc1, 256)
    return pl.pallas_call(
        functools.partial(_attn_cmp_kernel, ns=ns, chunk=chunk),
        grid=(s // QB,),
        in_specs=[pl.BlockSpec((QB, NSA_INNER), lambda i: (i, 0)),
                  pl.BlockSpec((1, G, nc1, hd), lambda i: (0, 0, 0, 0)),
                  pl.BlockSpec((1, G, nc1, hd), lambda i: (1, 0, 0, 0)),
                  pl.BlockSpec((QB, G * LANES), lambda i: (i, _col_block(layout, "nsa", G * LANES))),
                  pl.BlockSpec((nc1, ns), lambda i: (0, 0))],
        out_specs=[pl.BlockSpec((QB, NSA_INNER), lambda i: (i, 0)),
                   pl.BlockSpec((G, QB, nsp), lambda i: (0, i, 0))],
        out_shape=[jax.ShapeDtypeStruct((s, NSA_INNER), F32), jax.ShapeDtypeStruct((G, s, nsp), BF16)],
        scratch_shapes=[pltpu.VMEM((G * QB, ns), F32)],
        compiler_params=_params(("parallel",)),
        name="attn_cmp",
    )(q_raw, kvc, kvc, proj, jnp.asarray(overlap, BF16))


def _attn_sel_kernel(q_ref, sb_ref, ksa_ref, vs_ref, *rest, tk, slots, nwin):
    kw_refs, vw_refs = rest[:nwin], rest[nwin:2 * nwin]
    gt_ref, oc_ref, o_ref, qa_ref, m_ref, acc_ref = rest[2 * nwin:]
    QB, R, hd = Q_BLOCK, NSA_R, NSA_HEAD_DIM
    t0 = pl.program_id(1) * QB
    sb = sb_ref[0]
    for r in range(R):
        rows = slice(r * QB, (r + 1) * QB)
        for hf in range(qa_ref.shape[0]):
            qa_ref[hf, rows, :hd] = q_ref[:, r * hd:(r + 1) * hd]
            qa_ref[hf, rows, hd:] = sb[:, hf * slots:(hf + 1) * slots]
    m_ref[...] = jnp.full(m_ref.shape, NEG, F32)
    acc_ref[...] = jnp.zeros(acc_ref.shape, F32)
    tiles_per_half = slots * SEL_LEN // tk
    nlt = tk // LANES

    def scores(kt):
        k0 = pl.multiple_of(kt * tk, tk)
        return _dot_nt(qa_ref[kt // tiles_per_half], ksa_ref[0, pl.ds(k0, tk), :])

    def accumulate(kt, s):
        k0 = pl.multiple_of(kt * tk, tk)
        m_old = m_ref[...]
        mx = s[:, :LANES]
        for j in range(1, nlt):
            mx = jnp.maximum(mx, s[:, j * LANES:(j + 1) * LANES])
        m_new = jnp.maximum(m_old, jnp.max(mx, axis=-1, keepdims=True))
        alpha = jnp.exp2(m_old - m_new)
        p = jnp.exp2(s - jnp.concatenate([m_new] * nlt, axis=1)).astype(BF16)
        m_ref[...] = m_new
        acc_ref[...] = jnp.concatenate([alpha, alpha], axis=1) * acc_ref[...] + _dot(p, vs_ref[0, pl.ds(k0, tk), :])

    kt_last = (t0 + QB - 1) // tk

    def body(kt, s):
        s_next = scores(kt + 1)
        accumulate(kt, s)
        return s_next

    s_last = lax.fori_loop(0, kt_last, body, scores(0))
    causal = (kt_last * tk + lax.broadcasted_iota(jnp.int32, (QB, tk), 1)
              <= t0 + lax.broadcasted_iota(jnp.int32, (QB, tk), 0))
    accumulate(kt_last, jnp.where(causal[None], s_last.reshape(R, QB, tk), NEG).reshape(R * QB, tk))

    kw = jnp.concatenate([r[0] for r in kw_refs], axis=0)
    vw = jnp.concatenate([r[0] for r in vw_refs], axis=0)
    nk = nwin * QB
    kpos = t0 - WINDOW + lax.broadcasted_iota(jnp.int32, (QB, nk), 1)
    dlt = t0 + lax.broadcasted_iota(jnp.int32, (QB, nk), 0) - kpos
    vis = (dlt >= 0) & (dlt < WINDOW) & (kpos >= 0)
    gate = jax.nn.sigmoid(gt_ref[...])
    for r in range(R):
        rows = slice(r * QB, (r + 1) * QB)
        cols = slice(r * hd, (r + 1) * hd)
        sw = jnp.where(vis, _dot_nt(q_ref[:, cols], kw), NEG)
        ew = jnp.where(vis, jnp.exp2(sw - jnp.max(sw, axis=-1, keepdims=True)), 0.0)
        pw = ew / jnp.maximum(jnp.sum(ew, axis=-1, keepdims=True), 1e-30)
        ow = _dot(pw.astype(BF16), vw)
        osel = acc_ref[rows, :hd] / jnp.maximum(acc_ref[rows, hd:], 1e-30)
        o_ref[:, cols] = oc_ref[:, cols] + osel * gate[:, R + r:R + r + 1] + ow * gate[:, 2 * R + r:2 * R + r + 1]


def _attn_sel(q_rot, selb, ksa, vs, kw, vw, proj, layout, oc, slots, tk):
    s = q_rot.shape[0]
    G, QB, hd = NSA_KV_GROUPS, Q_BLOCK, NSA_HEAD_DIM
    gw = NSA_R * hd
    nsp = selb.shape[2]
    nwin = WINDOW // QB + 1
    win = [pl.BlockSpec((1, QB, hd), functools.partial(lambda g, i, j: (g, jnp.maximum(i - (nwin - 1) + j, 0), 0), j=j))
           for j in range(nwin)]
    qblk = pl.BlockSpec((QB, gw), lambda g, i: (i, g))
    return pl.pallas_call(
        functools.partial(_attn_sel_kernel, tk=tk, slots=slots, nwin=nwin),
        grid=(G, s // QB),
        in_specs=[qblk,
                  pl.BlockSpec((1, QB, nsp), lambda g, i: (g, i, 0)),
                  pl.BlockSpec((1, s, hd + slots), lambda g, i: (g, 0, 0)),
                  pl.BlockSpec((1, s, 2 * hd), lambda g, i: (g, 0, 0)),
                  *win, *win,
                  pl.BlockSpec((QB, LANES), lambda g, i: (i, _col_block(layout, "nsa", LANES) + g)),
                  qblk],
        out_specs=qblk,
        out_shape=jax.ShapeDtypeStruct((s, NSA_INNER), F32),
        scratch_shapes=[pltpu.VMEM((nsp // slots, NSA_R * QB, hd + slots), BF16),
                        pltpu.VMEM((NSA_R * QB, LANES), F32), pltpu.VMEM((NSA_R * QB, 2 * hd), F32)],
        compiler_params=_params(("parallel", "arbitrary")),
        name="attn_sel",
    )(q_rot, selb, ksa, vs, *([kw] * nwin), *([vw] * nwin), proj, oc)


def _merge_kernel(ya_ref, o_ref, zb_ref, wa_ref, wb_ref, ga_ref, gb_ref, m_ref, ob_ref):
    @pl.when(pl.program_id(1) == 0)
    def _():
        ob_ref[...] = (o_ref[...] * _silu(zb_ref[...])).astype(BF16)

    u_a = _dot(ya_ref[...], wa_ref[...])
    u_b = _dot(ob_ref[...], wb_ref[...])
    m_ref[...] = (jax.nn.sigmoid(ga_ref[...]) * u_a + jax.nn.sigmoid(gb_ref[...]) * u_b).astype(m_ref.dtype)


def _merge(y_a, o, proj, layout, wa, wb):
    s = y_a.shape[0]
    d = wa.shape[1]
    tm, tn = min(512, s), min(512, d)
    return pl.pallas_call(
        _merge_kernel,
        grid=(s // tm, d // tn),
        in_specs=[pl.BlockSpec((tm, SSM_INNER), lambda i, j: (i, 0)),
                  pl.BlockSpec((tm, NSA_INNER), lambda i, j: (i, 0)),
                  pl.BlockSpec((tm, NSA_INNER), lambda i, j: (i, _col_block(layout, "z_b", NSA_INNER))),
                  pl.BlockSpec((SSM_INNER, tn), lambda i, j: (0, j)),
                  pl.BlockSpec((NSA_INNER, tn), lambda i, j: (0, j)),
                  pl.BlockSpec((tm, tn), lambda i, j: (i, _col_block(layout, "g_a", tn) + j)),
                  pl.BlockSpec((tm, tn), lambda i, j: (i, _col_block(layout, "g_b", tn) + j))],
        out_specs=pl.BlockSpec((tm, tn), lambda i, j: (i, j)),
        out_shape=jax.ShapeDtypeStruct((s, d), BF16),
        scratch_shapes=[pltpu.VMEM((tm, NSA_INNER), BF16)],
        compiler_params=_params(("parallel", "arbitrary")),
        name="merge",
    )(y_a, o, proj, wa, wb, proj, proj)


def _out_kernel(m_ref, w_ref, x_ref, gate_ref, fnw_ref, o_ref, r_ref, *, nt):
    j = pl.program_id(1)
    r_ref[j] = x_ref[...] + gate_ref[...] * _dot(m_ref[...], w_ref[...])

    @pl.when(j == nt - 1)
    def _():
        tn = r_ref.shape[2]
        ss = sum(jnp.sum(r_ref[t] * r_ref[t], axis=-1, keepdims=True) for t in range(nt))
        inv = lax.rsqrt(ss / (nt * tn) + EPS)
        for t in range(nt):
            o_ref[:, t * tn:(t + 1) * tn] = r_ref[t] * inv * fnw_ref[:, t * tn:(t + 1) * tn]


def _out_proj(merged, w_out, x2, mod, fnw):
    s, d = x2.shape
    tm, tn = min(512, s), min(512, d)
    nt = d // tn
    return pl.pallas_call(
        functools.partial(_out_kernel, nt=nt),
        grid=(s // tm, nt),
        in_specs=[pl.BlockSpec((tm, d), lambda i, j: (i, 0)),
                  pl.BlockSpec((d, tn), lambda i, j: (0, j)),
                  pl.BlockSpec((tm, tn), lambda i, j: (i, j)),
                  pl.BlockSpec((1, tn), lambda i, j: (0, 2 * nt + j)),
                  pl.BlockSpec((1, d), lambda i, j: (0, 0))],
        out_specs=pl.BlockSpec((tm, d), lambda i, j: (i, 0)),
        out_shape=jax.ShapeDtypeStruct((s, d), F32),
        scratch_shapes=[pltpu.VMEM((nt, tm, tn), F32)],
        compiler_params=_params(("parallel", "arbitrary")),
        name="out_proj",
    )(merged, w_out, x2, mod, fnw)


def _layer(x2, c, w_ada, b_ada, norm_w, w_in, conv_w, conv_b, dt_bias, a_log, d_skip, ssm_norm_w, cmp_pe_k, cmp_pe_v,
           cmp_k_w1, cmp_k_w2, cmp_v_w1, cmp_v_w2, w_proj_a, w_proj_b, w_out, final_norm_w, cos, sin):
    s, d = x2.shape
    G, hd = NSA_KV_GROUPS, NSA_HEAD_DIM
    layout, width = _proj_layout(d)
    n_pad = -(-width // 512) * 512
    slots = SEL_SLOTS
    nsp = max(s // SEL_LEN, slots)
    assert nsp % slots == 0 and (slots * SEL_LEN) % SEL_KEY_TILE == 0 and s % SEL_KEY_TILE == 0

    mod = _ada_mod(c.reshape(d, 1), w_ada, b_ada[None, :])
    proj = _in_proj(_norm_mod(x2, norm_w[None, :], mod), _build_w_in(w_in, d, n_pad))

    y_a = _ssd(proj, layout, conv_w, conv_b, dt_bias, a_log, d_skip, ssm_norm_w)

    q_raw, q_rot, ksa, vs, kw, vw = _nsa_prep(proj, layout, cos, sin, slots)
    kv0 = layout["kv"]
    half_blocks = lambda t: jnp.transpose(t.reshape(s // CMP_STRIDE, CMP_STRIDE, G, hd), (2, 0, 1, 3)).reshape(
        G, s // CMP_STRIDE, CMP_STRIDE * hd)
    xkv = jnp.stack([half_blocks(proj[:, kv0:kv0 + KV_W]), half_blocks(proj[:, kv0 + KV_W:kv0 + 2 * KV_W])])
    pe = jnp.stack([cmp_pe_k.reshape(2, CMP_STRIDE * hd), cmp_pe_v.reshape(2, CMP_STRIDE * hd)])
    kvc = _nsa_cmp(xkv, pe, jnp.stack([cmp_k_w1, cmp_v_w1]).astype(BF16), jnp.stack([cmp_k_w2, cmp_v_w2]).astype(BF16))

    oc, selb = _attn_cmp(q_raw, kvc, proj, layout, nsp)
    o = _attn_sel(q_rot, selb, ksa, vs, kw, vw, proj, layout, oc, slots, SEL_KEY_TILE)

    merged = _merge(y_a, o, proj, layout, w_proj_a.astype(BF16), w_proj_b.astype(BF16))
    return _out_proj(merged, w_out.astype(BF16), x2, mod, final_norm_w[None, :])


def kernel(x, c, w_ada, b_ada, norm_w, w_in, conv_w, conv_b, dt_bias, a_log, d_skip, ssm_norm_w, cmp_pe_k, cmp_pe_v,
           cmp_k_w1, cmp_k_w2, cmp_v_w1, cmp_v_w2, w_proj_a, w_proj_b, w_out, final_norm_w):
    b, s, d = x.shape
    assert b == 1 and w_ada.shape[0] == 1, "one sequence, one layer"
    pos = jnp.arange(s, dtype=F32)
    inv_freq = ROPE_THETA ** (-jnp.arange(0, ROPE_DIM, 2, dtype=F32) / ROPE_DIM)
    ang = pos[:, None] * inv_freq[None, :]
    out = _layer(x[0], c, w_ada[0], b_ada[0], norm_w[0], w_in[0], conv_w[0], conv_b[0], dt_bias[0], a_log[0],
                 d_skip[0], ssm_norm_w[0], cmp_pe_k[0], cmp_pe_v[0], cmp_k_w1[0], cmp_k_w2[0], cmp_v_w1[0],
                 cmp_v_w2[0], w_proj_a[0], w_proj_b[0], w_out[0], final_norm_w, jnp.cos(ang), jnp.sin(ang))
    return out[None]
```

```python
import functools

import numpy as np
import jax
import jax.numpy as jnp
from jax import lax
from jax.experimental import pallas as pl
from jax.experimental.pallas import tpu as pltpu

F32 = jnp.float32
BF16 = jnp.bfloat16

EPS = 1e-6
BIG = 1e30
NEG = -1e30
REMOVED = -3e38

SSM_HEADS = 32
SSM_HEAD_DIM = 64
SSM_INNER = SSM_HEADS * SSM_HEAD_DIM
SSM_GROUPS = 8
SSM_STATE = 128
SSM_CONV = 4
SSM_CHUNK = 128
SSM_GN = SSM_GROUPS * SSM_STATE
SSM_XBC = SSM_INNER + 2 * SSM_GN
SSM_RP = SSM_INNER // SSM_GROUPS

NSA_HEADS = 16
NSA_KV_GROUPS = 2
NSA_R = NSA_HEADS // NSA_KV_GROUPS
NSA_HEAD_DIM = 128
NSA_INNER = NSA_HEADS * NSA_HEAD_DIM
KV_W = NSA_KV_GROUPS * NSA_HEAD_DIM
CMP_STRIDE = 16
CMP_LEN = 32
CMP_HIDDEN = 256
SEL_LEN = 64
SEL_TOP = 16
WINDOW = 512
Q_BLOCK = 128
ROPE_THETA = 500000.0
ROPE_DIM = NSA_HEAD_DIM // 4
ROPE_HALF = ROPE_DIM // 2

LANES = 128
SEL_SLOTS = 128
SEL_KEY_TILE = 512
SEL_COL_BLOCK = 256
SEL_LOOKAHEAD = 2
SEL_V_ROWS = NSA_HEAD_DIM + 16
LOG2E = 1.4426950408889634
VMEM_LIMIT = 56 * 1024 * 1024


def _params(sem, vmem=VMEM_LIMIT):
    return pltpu.CompilerParams(dimension_semantics=sem, vmem_limit_bytes=vmem)


def _tile(n, pref=512):
    t = min(pref, n)
    while n % t:
        t //= 2
    return t


def _col_block(layout, name, width):
    assert layout[name] % width == 0, (name, layout[name], width)
    return layout[name] // width


def _silu(v):
    return v * jax.nn.sigmoid(v)


def _softplus(v):
    return jnp.maximum(v, 0.0) + jnp.log1p(jnp.exp(-jnp.abs(v)))


def _split3(v):
    hi = v.astype(BF16)
    r = v - hi.astype(F32)
    mid = r.astype(BF16)
    lo = (r - mid.astype(F32)).astype(BF16)
    return hi, mid, lo


def _dot(a, b):
    return jnp.dot(a, b, preferred_element_type=F32)


def _dot_nt(a, b):
    return lax.dot_general(a, b, (((1,), (1,)), ((), ())), preferred_element_type=F32)


def _dot3_left(m01, v):
    hi, mid, lo = _split3(v)
    return _dot(m01, hi) + _dot(m01, mid) + _dot(m01, lo)


def _dot3_right(v, m01):
    hi, mid, lo = _split3(v)
    return _dot(hi, m01) + _dot(mid, m01) + _dot(lo, m01)


def _proj_layout(d_model):
    segs = [("xbc", SSM_XBC), ("g_a", d_model), ("g_b", d_model), ("z_a", SSM_INNER), ("q", NSA_INNER),
            ("z_b", NSA_INNER), ("kv", 6 * KV_W), ("nsa", NSA_KV_GROUPS * LANES), ("dt", LANES)]
    segs = sorted(segs, key=lambda s: -s[1])
    off, out = 0, {}
    for name, w in segs:
        out[name] = off
        off += w
    return out, off


def _src_offsets(d_model):
    splits = (SSM_INNER, SSM_XBC, SSM_HEADS, NSA_INNER, KV_W, KV_W, KV_W, KV_W, KV_W, KV_W,
              NSA_HEADS * 3, NSA_INNER, d_model, d_model)
    names = ("z_a", "xbc", "dt", "q", "kc", "vc", "ks", "vs", "kw", "vw", "nsa", "z_b", "g_a", "g_b")
    offs = np.concatenate([[0], np.cumsum(splits)])
    return {n: (int(offs[i]), int(splits[i])) for i, n in enumerate(names)}


def _build_w_in(w_in, d_model, n_pad):
    src = _src_offsets(d_model)
    layout, width = _proj_layout(d_model)

    def cols(name, w=None):
        a, n = src[name]
        return w_in[:, a:a + (n if w is None else w)]

    nsa = cols("nsa").reshape(d_model, NSA_KV_GROUPS, NSA_R, 3)
    nsa = jnp.transpose(nsa, (0, 1, 3, 2)).reshape(d_model, NSA_KV_GROUPS, 3 * NSA_R)
    nsa = jnp.pad(nsa, ((0, 0), (0, 0), (0, LANES - 3 * NSA_R))).reshape(d_model, NSA_KV_GROUPS * LANES)
    pieces = {
        "xbc": cols("xbc"), "g_a": cols("g_a"), "g_b": cols("g_b"), "z_a": cols("z_a"), "q": cols("q"),
        "z_b": cols("z_b"), "kv": cols("kc", 6 * KV_W), "nsa": nsa,
        "dt": jnp.pad(cols("dt"), ((0, 0), (0, LANES - SSM_HEADS))),
    }
    order = sorted(layout, key=lambda n: layout[n])
    parts = [pieces[n].astype(BF16) for n in order]
    if n_pad > width:
        parts.append(jnp.zeros((d_model, n_pad - width), BF16))
    return jnp.concatenate(parts, axis=1)


def _mod_kernel(c_ref, w_ref, b_ref, o_ref):
    sc = _silu(c_ref[...])
    o_ref[...] = jnp.sum(w_ref[...] * sc, axis=0, keepdims=True) + b_ref[...]


def _ada_mod(c_col, w_ada, b_ada):
    d, n = w_ada.shape
    tn = _tile(n)
    return pl.pallas_call(
        _mod_kernel,
        grid=(n // tn,),
        in_specs=[pl.BlockSpec((d, 1), lambda j: (0, 0)),
                  pl.BlockSpec((d, tn), lambda j: (0, j)),
                  pl.BlockSpec((1, tn), lambda j: (0, j))],
        out_specs=pl.BlockSpec((1, tn), lambda j: (0, j)),
        out_shape=jax.ShapeDtypeStruct((1, n), F32),
        compiler_params=_params(("arbitrary",)),
        name="ada_mod",
    )(c_col, w_ada, b_ada)


def _norm_mod_kernel(x_ref, nw_ref, shift_ref, scl_ref, h_ref):
    x = x_ref[...]
    y = x * lax.rsqrt(jnp.mean(x * x, axis=-1, keepdims=True) + EPS) * nw_ref[...]
    h_ref[...] = (y * (1.0 + scl_ref[...]) + shift_ref[...]).astype(BF16)


def _norm_mod(x2, norm_w, mod):
    s, d = x2.shape
    tm = _tile(s, 256)
    return pl.pallas_call(
        _norm_mod_kernel,
        grid=(s // tm,),
        in_specs=[pl.BlockSpec((tm, d), lambda i: (i, 0)),
                  pl.BlockSpec((1, d), lambda i: (0, 0)),
                  pl.BlockSpec((1, d), lambda i: (0, 0)),
                  pl.BlockSpec((1, d), lambda i: (0, 1))],
        out_specs=pl.BlockSpec((tm, d), lambda i: (i, 0)),
        out_shape=jax.ShapeDtypeStruct((s, d), BF16),
        compiler_params=_params(("parallel",)),
        name="norm_mod",
    )(x2, norm_w, mod, mod)


def _in_proj_kernel(h_ref, w_ref, o_ref):
    o_ref[...] = _dot(h_ref[...], w_ref[...])


def _in_proj(h, w_in_p):
    s, d = h.shape
    n = w_in_p.shape[1]
    tm, tn = _tile(s, 1024), _tile(n, 512)
    return pl.pallas_call(
        _in_proj_kernel,
        grid=(s // tm, n // tn),
        in_specs=[pl.BlockSpec((tm, d), lambda i, j: (i, 0)),
                  pl.BlockSpec((d, tn), lambda i, j: (0, j))],
        out_specs=pl.BlockSpec((tm, tn), lambda i, j: (i, j)),
        out_shape=jax.ShapeDtypeStruct((s, n), F32),
        compiler_params=_params(("parallel", "arbitrary")),
        name="in_proj",
    )(h, w_in_p)


def _ssd_kernel(xbc_ref, za_ref, dt_ref, cw_ref, cb_ref, dtb_ref, a_ref, dsk_ref, nw_ref, dtbs_ref, as_ref,
                e_ref, o_ref, xp_ref, st_ref, y_ref):
    L, N, P, RP = SSM_CHUNK, SSM_STATE, SSM_HEAD_DIM, SSM_RP

    @pl.when(pl.program_id(0) == 0)
    def _():
        st_ref[...] = jnp.zeros_like(st_ref)
        xp_ref[pl.ds(L, 8), :] = jnp.zeros((8, SSM_XBC), F32)

    xp_ref[pl.ds(0, 8), :] = xp_ref[pl.ds(L, 8), :]
    x = xbc_ref[...]
    xp_ref[pl.ds(8, L), :] = x
    cw = cw_ref[...]
    conv = (cb_ref[...] + cw[3:4] * x + cw[2:3] * xp_ref[pl.ds(7, L), :]
            + cw[1:2] * xp_ref[pl.ds(6, L), :] + cw[0:1] * xp_ref[pl.ds(5, L), :])
    act = _silu(conv)
    xs = act[:, :SSM_INNER]
    bm = act[:, SSM_INNER:SSM_INNER + SSM_GN]
    cm = act[:, SSM_INNER + SSM_GN:]

    row = lax.broadcasted_iota(jnp.int32, (L, L), 0)
    col = lax.broadcasted_iota(jnp.int32, (L, L), 1)
    causal = row >= col
    tril = causal.astype(BF16)

    dtr = dt_ref[...]
    dt = _softplus(_dot3_right(dtr, e_ref[...]) + dtb_ref[...])
    a_cs = _dot3_left(tril, dt * a_ref[...])
    a_cs_s = _dot3_left(tril, _softplus(dtr + dtbs_ref[...]) * as_ref[...])
    a_cs_t = a_cs_s.T

    xdt = xs * dt
    ea = jnp.exp(a_cs)
    a_last = a_cs[L - 1:L, :]
    cdec = jnp.exp(a_last)
    xdt_b = xdt.astype(BF16)
    xw_b = (xdt * jnp.exp(a_last - a_cs)).astype(BF16)

    for g in range(SSM_GROUPS):
        bg = bm[:, g * N:(g + 1) * N]
        cg_b = cm[:, g * N:(g + 1) * N].astype(BF16)
        cb = _dot_nt(cg_b, bg.astype(BF16))
        st = st_ref[g]
        y_off = _dot(cg_b, st.astype(BF16)) * ea[:, g * RP:(g + 1) * RP]
        for r in range(SSM_HEADS // SSM_GROUPS):
            h = g * (SSM_HEADS // SSM_GROUPS) + r
            seg = a_cs_s[:, h:h + 1] - a_cs_t[h:h + 1, :]
            dec = jnp.exp(jnp.where(causal, seg, NEG))
            y_d = _dot((cb * dec).astype(BF16), xdt_b[:, h * P:(h + 1) * P])
            y_ref[:, h * P:(h + 1) * P] = y_d + y_off[:, r * P:(r + 1) * P]
        new = _dot(bg.T.astype(BF16), xw_b[:, g * RP:(g + 1) * RP])
        st_ref[g] = st * cdec[:, g * RP:(g + 1) * RP] + new

    y = y_ref[...] + dsk_ref[...] * xs
    yg = y * _silu(za_ref[...])
    o_ref[...] = (yg * lax.rsqrt(jnp.mean(yg * yg, axis=-1, keepdims=True) + EPS) * nw_ref[...]).astype(o_ref.dtype)


def _ssd(proj, layout, conv_w, conv_b, dt_bias, a_log, d_skip, ssm_norm_w):
    s = proj.shape[0]
    L = SSM_CHUNK
    rep = lambda v: jnp.repeat(v.astype(F32), SSM_HEAD_DIM)[None, :]
    pad = lambda v: jnp.pad(v.astype(F32), (0, LANES - SSM_HEADS))[None, :]
    a = -jnp.exp(a_log.astype(F32))
    expand = np.zeros((LANES, SSM_INNER), np.float32)
    expand[np.arange(SSM_INNER) // SSM_HEAD_DIM, np.arange(SSM_INNER)] = 1.0
    const = lambda shape: pl.BlockSpec(shape, lambda c: (0,) * len(shape))
    return pl.pallas_call(
        _ssd_kernel,
        grid=(s // L,),
        in_specs=[pl.BlockSpec((L, SSM_XBC), lambda c: (c, _col_block(layout, "xbc", SSM_XBC))),
                  pl.BlockSpec((L, SSM_INNER), lambda c: (c, _col_block(layout, "z_a", SSM_INNER))),
                  pl.BlockSpec((L, LANES), lambda c: (c, _col_block(layout, "dt", LANES))),
                  const((SSM_CONV, SSM_XBC)), const((1, SSM_XBC)),
                  const((1, SSM_INNER)), const((1, SSM_INNER)), const((1, SSM_INNER)), const((1, SSM_INNER)),
                  const((1, LANES)), const((1, LANES)), const((LANES, SSM_INNER))],
        out_specs=pl.BlockSpec((L, SSM_INNER), lambda c: (c, 0)),
        out_shape=jax.ShapeDtypeStruct((s, SSM_INNER), BF16),
        scratch_shapes=[pltpu.VMEM((L + 8, SSM_XBC), F32),
                        pltpu.VMEM((SSM_GROUPS, SSM_STATE, SSM_RP), F32),
                        pltpu.VMEM((L, SSM_INNER), F32)],
        compiler_params=_params(("arbitrary",)),
        name="ssd",
    )(proj, proj, proj, conv_w, conv_b[None, :], rep(dt_bias), rep(a), rep(d_skip), ssm_norm_w[None, :],
      pad(dt_bias), pad(a), jnp.asarray(expand, BF16))


def _prep_kernel(q_ref, ks_ref, vsi_ref, kwi_ref, vwi_ref, cos_ref, sa_ref, sb_ref, qr_ref, qt_ref, ksa_ref, vt_ref,
                 kw_ref, vwt_ref, *, tm, slots):
    cosf, sa, sb = cos_ref[...], sa_ref[...], sb_ref[...]
    hd = NSA_HEAD_DIM

    def rope(t):
        return t * cosf + pltpu.roll(t, hd - ROPE_HALF, 1) * sa + pltpu.roll(t, ROPE_HALF, 1) * sb

    scale = hd ** -0.5 * LOG2E
    for h in range(NSA_HEADS):
        t = q_ref[:, h * hd:(h + 1) * hd]
        qr_ref[:, h * hd:(h + 1) * hd] = (t * scale).astype(BF16)
        qt_ref[h] = (rope(t) * scale).T.astype(BF16)

    pos = pl.program_id(0) * tm + lax.broadcasted_iota(jnp.int32, (tm, slots), 0)
    lane = lax.broadcasted_iota(jnp.int32, (tm, slots), 1)
    onehot = jnp.where((pos // SEL_LEN) % slots == lane, 1.0, 0.0).astype(BF16)
    for g in range(NSA_KV_GROUPS):
        cols = slice(g * hd, (g + 1) * hd)
        ksa_ref[g, :, :hd] = rope(ks_ref[:, cols]).astype(BF16)
        ksa_ref[g, :, hd:] = onehot
        vt_ref[g, 0, :hd, :] = vsi_ref[:, cols].T.astype(BF16)
        vt_ref[g, 0, hd:, :] = jnp.ones((SEL_V_ROWS - hd, tm), BF16)
        kw_ref[g] = rope(kwi_ref[:, cols]).astype(BF16)
        vwt_ref[g] = vwi_ref[:, cols].T.astype(BF16)


def _nsa_prep(proj, layout, cos, sin, slots, tk):
    s = proj.shape[0]
    tm = _tile(tk, 512)
    hd, G = NSA_HEAD_DIM, NSA_KV_GROUPS
    ones = jnp.ones((s, hd - ROPE_DIM), F32)
    zeros = lambda w: jnp.zeros((s, w), F32)
    cosf = jnp.concatenate([cos, cos, ones], axis=1)
    sa = jnp.concatenate([-sin, zeros(hd - ROPE_HALF)], axis=1)
    sb = jnp.concatenate([zeros(ROPE_HALF), sin, zeros(hd - ROPE_DIM)], axis=1)
    tab = pl.BlockSpec((tm, hd), lambda i: (i, 0))
    grp = lambda w: pl.BlockSpec((G, tm, w), lambda i: (0, i, 0))
    kvb = _col_block(layout, "kv", KV_W)
    return pl.pallas_call(
        functools.partial(_prep_kernel, tm=tm, slots=slots),
        grid=(s // tm,),
        in_specs=[pl.BlockSpec((tm, NSA_INNER), lambda i: (i, _col_block(layout, "q", NSA_INNER))),
                  *[pl.BlockSpec((tm, KV_W), functools.partial(lambda i, k: (i, k), k=kvb + k)) for k in (2, 3, 4, 5)],
                  tab, tab, tab],
        out_specs=[pl.BlockSpec((tm, NSA_INNER), lambda i: (i, 0)),
                   pl.BlockSpec((NSA_HEADS, hd, tm), lambda i: (0, 0, i)),
                   grp(hd + slots),
                   pl.BlockSpec((G, 1, SEL_V_ROWS, tm), lambda i: (0, i // (tk // tm), 0, i % (tk // tm))),
                   grp(hd),
                   pl.BlockSpec((G, hd, tm), lambda i: (0, 0, i))],
        out_shape=[jax.ShapeDtypeStruct((s, NSA_INNER), BF16), jax.ShapeDtypeStruct((NSA_HEADS, hd, s), BF16),
                   jax.ShapeDtypeStruct((G, s, hd + slots), BF16),
                   jax.ShapeDtypeStruct((G, s // tk, SEL_V_ROWS, tk), BF16),
                   jax.ShapeDtypeStruct((G, s, hd), BF16), jax.ShapeDtypeStruct((G, hd, s), BF16)],
        compiler_params=_params(("parallel",)),
        name="nsa_prep",
    )(proj, proj, proj, proj, proj, cosf, sa, sb)


def _cmp_kernel(x_ref, pe_ref, w1_ref, w2_ref, o_ref):
    x = x_ref[0, 0]
    nc1, half = x.shape
    a = _dot((x + pe_ref[0, 0:1]).astype(BF16), w1_ref[0, :half])
    b = _dot((x + pe_ref[0, 1:2]).astype(BF16), w1_ref[0, half:])
    pre = a + pltpu.roll(b, nc1 - 1, 0)
    hdn = 0.5 * pre * (1.0 + jnp.tanh(0.7978845608028654 * (pre + 0.044715 * pre * pre * pre)))
    o_ref[0, 0] = _dot(hdn.astype(BF16), w2_ref[0]).astype(o_ref.dtype)


def _nsa_cmp(xkv, pe, w1, w2):
    _, G, nc1, half = xkv.shape
    return pl.pallas_call(
        _cmp_kernel,
        grid=(2, G),
        in_specs=[pl.BlockSpec((1, 1, nc1, half), lambda t, g: (t, g, 0, 0)),
                  pl.BlockSpec((1, 2, half), lambda t, g: (t, 0, 0)),
                  pl.BlockSpec((1, 2 * half, CMP_HIDDEN), lambda t, g: (t, 0, 0)),
                  pl.BlockSpec((1, CMP_HIDDEN, NSA_HEAD_DIM), lambda t, g: (t, 0, 0))],
        out_specs=pl.BlockSpec((1, 1, nc1, NSA_HEAD_DIM), lambda t, g: (t, g, 0, 0)),
        out_shape=jax.ShapeDtypeStruct((2, G, nc1, NSA_HEAD_DIM), BF16),
        compiler_params=_params(("parallel", "parallel")),
        name="nsa_cmp",
    )(xkv, pe, w1, w2)


def _stack_heads(q):
    return jnp.concatenate([q[:, r * NSA_HEAD_DIM:(r + 1) * NSA_HEAD_DIM] for r in range(NSA_R)], axis=0)


def _attn_cmp_kernel(q_ref, kc_ref, vc_ref, gt_ref, ov_ref, oc_ref, sb_ref, imp_ref, *, ns, chunk):
    QB, R, hd, G = Q_BLOCK, NSA_R, NSA_HEAD_DIM, NSA_KV_GROUPS
    i = pl.program_id(0)
    t0 = i * QB
    nc1 = kc_ref.shape[2]
    gate = jax.nn.sigmoid(gt_ref[...])

    def branch(nk):
        pos = t0 + lax.broadcasted_iota(jnp.int32, (QB, nk), 0)
        kend = lax.broadcasted_iota(jnp.int32, (QB, nk), 1) * CMP_STRIDE + (CMP_LEN - 1)
        vis = (kend <= pos)[None]
        for g in range(G):
            q = q_ref[:, g * R * hd:(g + 1) * R * hd]
            sm = jnp.where(vis, _dot_nt(_stack_heads(q), kc_ref[0, g, :nk, :]).reshape(R, QB, nk), NEG)
            e = jnp.where(vis, jnp.exp2(sm - jnp.max(sm, axis=-1, keepdims=True)), 0.0)
            p = e / jnp.maximum(jnp.sum(e, axis=-1, keepdims=True), 1e-30)
            oc = _dot(p.reshape(R * QB, nk).astype(BF16), vc_ref[0, g, :nk, :])
            for r in range(R):
                h = g * R + r
                oc_ref[:, h * hd:(h + 1) * hd] = oc[r * QB:(r + 1) * QB] * gate[:, g * LANES + r:g * LANES + r + 1]
            imp_ref[g * QB:(g + 1) * QB, :] = _dot3_right(jnp.sum(p, axis=0), ov_ref[:nk, :])

    need = (i * (QB // CMP_STRIDE) + (QB - CMP_LEN) // CMP_STRIDE + chunk) // chunk
    for n in range(1, nc1 // chunk + 1):
        pl.when(need == n)(functools.partial(branch, n * chunk))

    shape = (G * QB, ns)
    jb = lax.broadcasted_iota(jnp.int32, shape, 1)
    row = lax.broadcasted_iota(jnp.int32, shape, 0)
    pq = t0 + jnp.where(row >= QB, row - QB, row)
    cur = pq // SEL_LEN
    eligible = jb * SEL_LEN <= pq
    forced = (jb == 0) | (jb == cur) | (jb == cur - 1)
    score = jnp.where(forced, REMOVED, jnp.where(eligible, imp_ref[...], -BIG))
    jbf = jb.astype(F32)

    def take(_, carry):
        score, sel = carry
        mx = jnp.max(score, axis=-1, keepdims=True)
        idx = jnp.min(jnp.where(score == mx, jbf, float(ns)), axis=-1, keepdims=True)
        pick = jbf == idx
        return jnp.where(pick, REMOVED, score), jnp.where(pick, 1.0, sel)

    _, sel = lax.fori_loop(0, SEL_TOP - 3, take, (score, jnp.where(forced, 1.0, 0.0)))
    bias = jnp.where((sel > 0.0) & eligible, 0.0, NEG).astype(BF16)
    nsp = sb_ref.shape[2]
    for g in range(G):
        sb_ref[g, :, :ns] = bias[g * QB:(g + 1) * QB]
        if nsp > ns:
            sb_ref[g, :, ns:] = jnp.full((QB, nsp - ns), NEG, BF16)


def _attn_cmp(q_raw, kvc, proj, layout, nsp):
    s = q_raw.shape[0]
    G, QB, hd = NSA_KV_GROUPS, Q_BLOCK, NSA_HEAD_DIM
    nc1 = kvc.shape[2]
    ns = s // SEL_LEN
    ic = np.arange(nc1)[:, None]
    jc = np.arange(ns)[None, :]
    overlap = ((ic * CMP_STRIDE < (jc + 1) * SEL_LEN) & (ic * CMP_STRIDE + CMP_LEN > jc * SEL_LEN)
               & (ic < nc1 - 1)).astype(np.float32)
    assert ns >= SEL_TOP
    chunk = _tile(nc1, 256)
    return pl.pallas_call(
        functools.partial(_attn_cmp_kernel, ns=ns, chunk=chunk),
        grid=(s // QB,),
        in_specs=[pl.BlockSpec((QB, NSA_INNER), lambda i: (i, 0)),
                  pl.BlockSpec((1, G, nc1, hd), lambda i: (0, 0, 0, 0)),
                  pl.BlockSpec((1, G, nc1, hd), lambda i: (1, 0, 0, 0)),
                  pl.BlockSpec((QB, G * LANES), lambda i: (i, _col_block(layout, "nsa", G * LANES))),
                  pl.BlockSpec((nc1, ns), lambda i: (0, 0))],
        out_specs=[pl.BlockSpec((QB, NSA_INNER), lambda i: (i, 0)),
                   pl.BlockSpec((G, QB, nsp), lambda i: (0, i, 0))],
        out_shape=[jax.ShapeDtypeStruct((s, NSA_INNER), F32), jax.ShapeDtypeStruct((G, s, nsp), BF16)],
        scratch_shapes=[pltpu.VMEM((G * QB, ns), F32)],
        compiler_params=_params(("parallel",)),
        name="attn_cmp",
    )(q_raw, kvc, kvc, proj, jnp.asarray(overlap, BF16))


def _attn_sel_kernel(qt_ref, sb_ref, ksa_ref, vt_ref, *rest, tk, slots, nwin, cb, look):
    kw_refs, vwt_refs = rest[:nwin], rest[nwin:2 * nwin]
    gt_ref, oc_ref, zb_ref, o_ref, qa_ref, m_ref, acc_ref = rest[2 * nwin:]
    QB, R, hd = Q_BLOCK, NSA_R, NSA_HEAD_DIM
    RQ = R * QB
    t0 = pl.program_id(1) * QB
    sb = sb_ref[0].astype(F32)
    for hf in range(qa_ref.shape[0]):
        sbt = sb[:, hf * slots:(hf + 1) * slots].T.astype(BF16)
        for r in range(R):
            qa_ref[hf, :hd, r * QB:(r + 1) * QB] = qt_ref[r]
            qa_ref[hf, hd:, r * QB:(r + 1) * QB] = sbt
    m_ref[...] = jnp.full(m_ref.shape, NEG, F32)
    acc_ref[...] = jnp.zeros(acc_ref.shape, F32)
    tiles_per_half = slots * SEL_LEN // tk

    ncb = RQ // cb

    def scores(kt, c):
        k0 = pl.multiple_of(kt * tk, tk)
        qa = qa_ref[kt // tiles_per_half, :, c * cb:(c + 1) * cb]
        return _dot(ksa_ref[0, pl.ds(k0, tk), :], qa)

    def accumulate(kt, c, s):
        cols = slice(c * cb, (c + 1) * cb)
        m_old = m_ref[:, cols]
        m_new = jnp.maximum(m_old, jnp.max(s, axis=0, keepdims=True))
        alpha = jnp.exp2(m_old - m_new)
        p = jnp.exp2(s - m_new).astype(BF16)
        m_ref[:, cols] = m_new
        acc_ref[:, cols] = alpha * acc_ref[:, cols] + _dot(vt_ref[0, kt], p)

    kt_last = (t0 + QB - 1) // tk

    def body(kt, pend):
        pend = list(pend)
        for c in range(ncb):
            ahead = c + look
            pend.append(scores(kt, ahead) if ahead < ncb else scores(kt + 1, ahead - ncb))
            accumulate(kt, c, pend.pop(0))
        return tuple(pend)

    pend = list(lax.fori_loop(0, kt_last, body, tuple(scores(0, c) for c in range(look))))
    qpos = t0 + (lax.broadcasted_iota(jnp.int32, (tk, cb), 1) & (QB - 1))
    causal = kt_last * tk + lax.broadcasted_iota(jnp.int32, (tk, cb), 0) <= qpos
    for c in range(ncb):
        if c + look < ncb:
            pend.append(scores(kt_last, c + look))
        accumulate(kt_last, c, jnp.where(causal, pend.pop(0), NEG))

    kw = jnp.concatenate([r[0] for r in kw_refs], axis=0)
    vwt = jnp.concatenate([r[0] for r in vwt_refs], axis=1)
    nk = nwin * QB
    kpos = t0 - WINDOW + lax.broadcasted_iota(jnp.int32, (nk, cb), 0)
    dlt = t0 + (lax.broadcasted_iota(jnp.int32, (nk, cb), 1) & (QB - 1)) - kpos
    vis = (dlt >= 0) & (dlt < WINDOW) & (kpos >= 0)
    gate = jax.nn.sigmoid(gt_ref[...])
    wscores = lambda c: _dot(kw, qa_ref[0, :hd, c * cb:(c + 1) * cb])
    pend = [wscores(c) for c in range(min(look, ncb))]
    for c in range(ncb):
        if c + look < ncb:
            pend.append(wscores(c + look))
        sw = jnp.where(vis, pend.pop(0), NEG)
        ew = jnp.where(vis, jnp.exp2(sw - jnp.max(sw, axis=0, keepdims=True)), 0.0)
        owt = _dot(vwt, ew.astype(BF16)) / jnp.maximum(jnp.sum(ew, axis=0, keepdims=True), 1e-30)
        ost = acc_ref[:hd, c * cb:(c + 1) * cb] / jnp.maximum(acc_ref[hd:hd + 1, c * cb:(c + 1) * cb], 1e-30)
        for j in range(cb // QB):
            r = c * (cb // QB) + j
            cols = slice(r * hd, (r + 1) * hd)
            lanes = slice(j * QB, (j + 1) * QB)
            o = (oc_ref[:, cols] + ost[:, lanes].T * gate[:, R + r:R + r + 1]
                 + owt[:, lanes].T * gate[:, 2 * R + r:2 * R + r + 1])
            o_ref[:, cols] = (o * _silu(zb_ref[:, cols])).astype(o_ref.dtype)


def _attn_sel(q_rot_t, selb, ksa, vt, kw, vwt, proj, layout, oc, slots, tk):
    G, QB, hd, R = NSA_KV_GROUPS, Q_BLOCK, NSA_HEAD_DIM, NSA_R
    s = ksa.shape[1]
    nsp = selb.shape[2]
    nwin = WINDOW // QB + 1
    wblk = lambda j: (lambda g, i: (g, jnp.maximum(i - (nwin - 1) + j, 0), 0))
    wblk_t = lambda j: (lambda g, i: (g, 0, jnp.maximum(i - (nwin - 1) + j, 0)))
    oblk = pl.BlockSpec((QB, R * hd), lambda g, i: (i, g))
    return pl.pallas_call(
        functools.partial(_attn_sel_kernel, tk=tk, slots=slots, nwin=nwin, cb=SEL_COL_BLOCK, look=SEL_LOOKAHEAD),
        grid=(G, s // QB),
        in_specs=[pl.BlockSpec((R, hd, QB), lambda g, i: (g, 0, i)),
                  pl.BlockSpec((1, QB, nsp), lambda g, i: (g, i, 0)),
                  pl.BlockSpec((1, s, hd + slots), lambda g, i: (g, 0, 0)),
                  pl.BlockSpec((1, s // tk, SEL_V_ROWS, tk), lambda g, i: (g, 0, 0, 0)),
                  *[pl.BlockSpec((1, QB, hd), wblk(j)) for j in range(nwin)],
                  *[pl.BlockSpec((1, hd, QB), wblk_t(j)) for j in range(nwin)],
                  pl.BlockSpec((QB, LANES), lambda g, i: (i, _col_block(layout, "nsa", LANES) + g)),
                  oblk,
                  pl.BlockSpec((QB, R * hd), lambda g, i: (i, _col_block(layout, "z_b", R * hd) + g))],
        out_specs=oblk,
        out_shape=jax.ShapeDtypeStruct((s, NSA_INNER), BF16),
        scratch_shapes=[pltpu.VMEM((nsp // slots, hd + slots, R * QB), BF16),
                        pltpu.VMEM((1, R * QB), F32), pltpu.VMEM((SEL_V_ROWS, R * QB), F32)],
        compiler_params=_params(("parallel", "arbitrary")),
        name="attn_sel",
    )(q_rot_t, selb, ksa, vt, *([kw] * nwin), *([vwt] * nwin), proj, oc, proj)


def _merge_kernel(ya_ref, ob_ref, wa_ref, wb_ref, ga_ref, gb_ref, m_ref):
    u_a = _dot(ya_ref[...], wa_ref[...])
    u_b = _dot(ob_ref[...], wb_ref[...])
    m_ref[...] = (jax.nn.sigmoid(ga_ref[...]) * u_a + jax.nn.sigmoid(gb_ref[...]) * u_b).astype(m_ref.dtype)


def _merge(y_a, ob, proj, layout, wa, wb):
    s = y_a.shape[0]
    d = wa.shape[1]
    tm, tn = _tile(s, 1024), _tile(d, 512)
    return pl.pallas_call(
        _merge_kernel,
        grid=(s // tm, d // tn),
        in_specs=[pl.BlockSpec((tm, SSM_INNER), lambda i, j: (i, 0)),
                  pl.BlockSpec((tm, NSA_INNER), lambda i, j: (i, 0)),
                  pl.BlockSpec((SSM_INNER, tn), lambda i, j: (0, j)),
                  pl.BlockSpec((NSA_INNER, tn), lambda i, j: (0, j)),
                  pl.BlockSpec((tm, tn), lambda i, j: (i, _col_block(layout, "g_a", tn) + j)),
                  pl.BlockSpec((tm, tn), lambda i, j: (i, _col_block(layout, "g_b", tn) + j))],
        out_specs=pl.BlockSpec((tm, tn), lambda i, j: (i, j)),
        out_shape=jax.ShapeDtypeStruct((s, d), BF16),
        compiler_params=_params(("parallel", "arbitrary")),
        name="merge",
    )(y_a, ob, wa, wb, proj, proj)


def _out_kernel(m_ref, w_ref, x_ref, gate_ref, fnw_ref, o_ref, r_ref, *, nt):
    j = pl.program_id(1)
    r_ref[j] = x_ref[...] + gate_ref[...] * _dot(m_ref[...], w_ref[...])

    @pl.when(j == nt - 1)
    def _():
        tn = r_ref.shape[2]
        ss = sum(jnp.sum(r_ref[t] * r_ref[t], axis=-1, keepdims=True) for t in range(nt))
        inv = lax.rsqrt(ss / (nt * tn) + EPS)
        for t in range(nt):
            o_ref[:, t * tn:(t + 1) * tn] = r_ref[t] * inv * fnw_ref[:, t * tn:(t + 1) * tn]


def _out_proj(merged, w_out, x2, mod, fnw):
    s, d = x2.shape
    tm, tn = min(512, s), min(512, d)
    nt = d // tn
    return pl.pallas_call(
        functools.partial(_out_kernel, nt=nt),
        grid=(s // tm, nt),
        in_specs=[pl.BlockSpec((tm, d), lambda i, j: (i, 0)),
                  pl.BlockSpec((d, tn), lambda i, j: (0, j)),
                  pl.BlockSpec((tm, tn), lambda i, j: (i, j)),
                  pl.BlockSpec((1, tn), lambda i, j: (0, 2 * nt + j)),
                  pl.BlockSpec((1, d), lambda i, j: (0, 0))],
        out_specs=pl.BlockSpec((tm, d), lambda i, j: (i, 0)),
        out_shape=jax.ShapeDtypeStruct((s, d), F32),
        scratch_shapes=[pltpu.VMEM((nt, tm, tn), F32)],
        compiler_params=_params(("parallel", "arbitrary")),
        name="out_proj",
    )(merged, w_out, x2, mod, fnw)


def _layer(x2, c, w_ada, b_ada, norm_w, w_in, conv_w, conv_b, dt_bias, a_log, d_skip, ssm_norm_w, cmp_pe_k, cmp_pe_v,
           cmp_k_w1, cmp_k_w2, cmp_v_w1, cmp_v_w2, w_proj_a, w_proj_b, w_out, final_norm_w, cos, sin):
    s, d = x2.shape
    G, hd = NSA_KV_GROUPS, NSA_HEAD_DIM
    layout, width = _proj_layout(d)
    n_pad = -(-width // 512) * 512
    slots = SEL_SLOTS
    nsp = max(s // SEL_LEN, slots)
    assert nsp % slots == 0 and (slots * SEL_LEN) % SEL_KEY_TILE == 0 and s % SEL_KEY_TILE == 0

    mod = _ada_mod(c.reshape(d, 1), w_ada, b_ada[None, :])
    proj = _in_proj(_norm_mod(x2, norm_w[None, :], mod), _build_w_in(w_in, d, n_pad))

    y_a = _ssd(proj, layout, conv_w, conv_b, dt_bias, a_log, d_skip, ssm_norm_w)

    q_raw, q_rot_t, ksa, vt, kw, vwt = _nsa_prep(proj, layout, cos, sin, slots, SEL_KEY_TILE)
    kv0 = layout["kv"]
    half_blocks = lambda t: jnp.transpose(t.reshape(s // CMP_STRIDE, CMP_STRIDE, G, hd), (2, 0, 1, 3)).reshape(
        G, s // CMP_STRIDE, CMP_STRIDE * hd)
    xkv = jnp.stack([half_blocks(proj[:, kv0:kv0 + KV_W]), half_blocks(proj[:, kv0 + KV_W:kv0 + 2 * KV_W])])
    pe = jnp.stack([cmp_pe_k.reshape(2, CMP_STRIDE * hd), cmp_pe_v.reshape(2, CMP_STRIDE * hd)])
    kvc = _nsa_cmp(xkv, pe, jnp.stack([cmp_k_w1, cmp_v_w1]).astype(BF16), jnp.stack([cmp_k_w2, cmp_v_w2]).astype(BF16))

    oc, selb = _attn_cmp(q_raw, kvc, proj, layout, nsp)
    o = _attn_sel(q_rot_t, selb, ksa, vt, kw, vwt, proj, layout, oc, slots, SEL_KEY_TILE)

    merged = _merge(y_a, o, proj, layout, w_proj_a.astype(BF16), w_proj_b.astype(BF16))
    return _out_proj(merged, w_out.astype(BF16), x2, mod, final_norm_w[None, :])


def kernel(x, c, w_ada, b_ada, norm_w, w_in, conv_w, conv_b, dt_bias, a_log, d_skip, ssm_norm_w, cmp_pe_k, cmp_pe_v,
           cmp_k_w1, cmp_k_w2, cmp_v_w1, cmp_v_w2, w_proj_a, w_proj_b, w_out, final_norm_w):
    b, s, d = x.shape
    assert b == 1 and w_ada.shape[0] == 1, "one sequence, one layer"
    pos = jnp.arange(s, dtype=F32)
    inv_freq = ROPE_THETA ** (-jnp.arange(0, ROPE_DIM, 2, dtype=F32) / ROPE_DIM)
    ang = pos[:, None] * inv_freq[None, :]
    out = _layer(x[0], c, w_ada[0], b_ada[0], norm_w[0], w_in[0], conv_w[0], conv_b[0], dt_bias[0], a_log[0],
                 d_skip[0], ssm_norm_w[0], cmp_pe_k[0], cmp_pe_v[0], cmp_k_w1[0], cmp_k_w2[0], cmp_v_w1[0],
                 cmp_v_w2[0], w_proj_a[0], w_proj_b[0], w_out[0], final_norm_w, jnp.cos(ang), jnp.sin(ang))
    return out[None]
```

```python
import functools

import numpy as np
import jax
import jax.numpy as jnp
from jax import lax
from jax.experimental import pallas as pl
from jax.experimental.pallas import tpu as pltpu

F32 = jnp.float32
BF16 = jnp.bfloat16

EPS = 1e-6
BIG = 1e30
NEG = -1e30
REMOVED = -3e38

SSM_HEADS = 32
SSM_HEAD_DIM = 64
SSM_INNER = SSM_HEADS * SSM_HEAD_DIM
SSM_GROUPS = 8
SSM_STATE = 128
SSM_CONV = 4
SSM_CHUNK = 128
SSM_GN = SSM_GROUPS * SSM_STATE
SSM_XBC = SSM_INNER + 2 * SSM_GN
SSM_RP = SSM_INNER // SSM_GROUPS

NSA_HEADS = 16
NSA_KV_GROUPS = 2
NSA_R = NSA_HEADS // NSA_KV_GROUPS
NSA_HEAD_DIM = 128
NSA_INNER = NSA_HEADS * NSA_HEAD_DIM
KV_W = NSA_KV_GROUPS * NSA_HEAD_DIM
CMP_STRIDE = 16
CMP_LEN = 32
CMP_HIDDEN = 256
SEL_LEN = 64
SEL_TOP = 16
WINDOW = 512
Q_BLOCK = 128
ROPE_THETA = 500000.0
ROPE_DIM = NSA_HEAD_DIM // 4
ROPE_HALF = ROPE_DIM // 2

LANES = 128
SEL_SLOTS = 128
SEL_KEY_TILE = 512
SEL_COL_BLOCK = 256
SEL_LOOKAHEAD = 2
SEL_UNROLL = 4
SEL_V_ROWS = NSA_HEAD_DIM + 16
LOG2E = 1.4426950408889634
VMEM_LIMIT = 56 * 1024 * 1024


def _params(sem, vmem=VMEM_LIMIT):
    return pltpu.CompilerParams(dimension_semantics=sem, vmem_limit_bytes=vmem)


def _tile(n, pref=512):
    t = min(pref, n)
    while n % t:
        t //= 2
    return t


def _col_block(layout, name, width):
    assert layout[name] % width == 0, (name, layout[name], width)
    return layout[name] // width


def _silu(v):
    return v * jax.nn.sigmoid(v)


def _softplus(v):
    return jnp.maximum(v, 0.0) + jnp.log1p(jnp.exp(-jnp.abs(v)))


def _split3(v):
    hi = v.astype(BF16)
    r = v - hi.astype(F32)
    mid = r.astype(BF16)
    lo = (r - mid.astype(F32)).astype(BF16)
    return hi, mid, lo


def _dot(a, b):
    return jnp.dot(a, b, preferred_element_type=F32)


def _dot_nt(a, b):
    return lax.dot_general(a, b, (((1,), (1,)), ((), ())), preferred_element_type=F32)


def _dot3_left(m01, v):
    hi, mid, lo = _split3(v)
    return _dot(m01, hi) + _dot(m01, mid) + _dot(m01, lo)


def _dot3_right(v, m01):
    hi, mid, lo = _split3(v)
    return _dot(hi, m01) + _dot(mid, m01) + _dot(lo, m01)


def _proj_layout(d_model):
    segs = [("xbc", SSM_XBC), ("g_a", d_model), ("g_b", d_model), ("z_a", SSM_INNER), ("q", NSA_INNER),
            ("z_b", NSA_INNER), ("kv", 6 * KV_W), ("nsa", NSA_KV_GROUPS * LANES), ("dt", LANES)]
    segs = sorted(segs, key=lambda s: -s[1])
    off, out = 0, {}
    for name, w in segs:
        out[name] = off
        off += w
    return out, off


def _src_offsets(d_model):
    splits = (SSM_INNER, SSM_XBC, SSM_HEADS, NSA_INNER, KV_W, KV_W, KV_W, KV_W, KV_W, KV_W,
              NSA_HEADS * 3, NSA_INNER, d_model, d_model)
    names = ("z_a", "xbc", "dt", "q", "kc", "vc", "ks", "vs", "kw", "vw", "nsa", "z_b", "g_a", "g_b")
    offs = np.concatenate([[0], np.cumsum(splits)])
    return {n: (int(offs[i]), int(splits[i])) for i, n in enumerate(names)}


def _build_w_in(w_in, d_model, n_pad):
    src = _src_offsets(d_model)
    layout, width = _proj_layout(d_model)

    def cols(name, w=None):
        a, n = src[name]
        return w_in[:, a:a + (n if w is None else w)]

    nsa = cols("nsa").reshape(d_model, NSA_KV_GROUPS, NSA_R, 3)
    nsa = jnp.transpose(nsa, (0, 1, 3, 2)).reshape(d_model, NSA_KV_GROUPS, 3 * NSA_R)
    nsa = jnp.pad(nsa, ((0, 0), (0, 0), (0, LANES - 3 * NSA_R))).reshape(d_model, NSA_KV_GROUPS * LANES)
    pieces = {
        "xbc": cols("xbc"), "g_a": cols("g_a"), "g_b": cols("g_b"), "z_a": cols("z_a"), "q": cols("q"),
        "z_b": cols("z_b"), "kv": cols("kc", 6 * KV_W), "nsa": nsa,
        "dt": jnp.pad(cols("dt"), ((0, 0), (0, LANES - SSM_HEADS))),
    }
    order = sorted(layout, key=lambda n: layout[n])
    parts = [pieces[n].astype(BF16) for n in order]
    if n_pad > width:
        parts.append(jnp.zeros((d_model, n_pad - width), BF16))
    return jnp.concatenate(parts, axis=1)


def _mod_kernel(c_ref, w_ref, b_ref, o_ref):
    sc = _silu(c_ref[...])
    o_ref[...] = jnp.sum(w_ref[...] * sc, axis=0, keepdims=True) + b_ref[...]


def _ada_mod(c_col, w_ada, b_ada):
    d, n = w_ada.shape
    tn = _tile(n)
    return pl.pallas_call(
        _mod_kernel,
        grid=(n // tn,),
        in_specs=[pl.BlockSpec((d, 1), lambda j: (0, 0)),
                  pl.BlockSpec((d, tn), lambda j: (0, j)),
                  pl.BlockSpec((1, tn), lambda j: (0, j))],
        out_specs=pl.BlockSpec((1, tn), lambda j: (0, j)),
        out_shape=jax.ShapeDtypeStruct((1, n), F32),
        compiler_params=_params(("arbitrary",)),
        name="ada_mod",
    )(c_col, w_ada, b_ada)


def _norm_mod_kernel(x_ref, nw_ref, shift_ref, scl_ref, h_ref):
    x = x_ref[...]
    y = x * lax.rsqrt(jnp.mean(x * x, axis=-1, keepdims=True) + EPS) * nw_ref[...]
    h_ref[...] = (y * (1.0 + scl_ref[...]) + shift_ref[...]).astype(BF16)


def _norm_mod(x2, norm_w, mod):
    s, d = x2.shape
    tm = _tile(s, 256)
    return pl.pallas_call(
        _norm_mod_kernel,
        grid=(s // tm,),
        in_specs=[pl.BlockSpec((tm, d), lambda i: (i, 0)),
                  pl.BlockSpec((1, d), lambda i: (0, 0)),
                  pl.BlockSpec((1, d), lambda i: (0, 0)),
                  pl.BlockSpec((1, d), lambda i: (0, 1))],
        out_specs=pl.BlockSpec((tm, d), lambda i: (i, 0)),
        out_shape=jax.ShapeDtypeStruct((s, d), BF16),
        compiler_params=_params(("parallel",)),
        name="norm_mod",
    )(x2, norm_w, mod, mod)


def _in_proj_kernel(h_ref, w_ref, o_ref):
    o_ref[...] = _dot(h_ref[...], w_ref[...])


def _in_proj(h, w_in_p):
    s, d = h.shape
    n = w_in_p.shape[1]
    tm, tn = _tile(s, 1024), _tile(n, 1024)
    return pl.pallas_call(
        _in_proj_kernel,
        grid=(s // tm, n // tn),
        in_specs=[pl.BlockSpec((tm, d), lambda i, j: (i, 0)),
                  pl.BlockSpec((d, tn), lambda i, j: (0, j))],
        out_specs=pl.BlockSpec((tm, tn), lambda i, j: (i, j)),
        out_shape=jax.ShapeDtypeStruct((s, n), F32),
        compiler_params=_params(("parallel", "arbitrary")),
        name="in_proj",
    )(h, w_in_p)


def _ssd_kernel(xbc_ref, za_ref, dt_ref, cw_ref, cb_ref, dsk_ref, nw_ref, dtbs_ref, as_ref,
                e_ref, o_ref, xp_ref, st_ref, y_ref):
    L, N, P, RP = SSM_CHUNK, SSM_STATE, SSM_HEAD_DIM, SSM_RP

    @pl.when(pl.program_id(0) == 0)
    def _():
        st_ref[...] = jnp.zeros_like(st_ref)
        xp_ref[pl.ds(L, 8), :] = jnp.zeros((8, SSM_XBC), F32)

    xp_ref[pl.ds(0, 8), :] = xp_ref[pl.ds(L, 8), :]
    x = xbc_ref[...]
    xp_ref[pl.ds(8, L), :] = x
    cw = cw_ref[...]
    conv = (cb_ref[...] + cw[3:4] * x + cw[2:3] * xp_ref[pl.ds(7, L), :]
            + cw[1:2] * xp_ref[pl.ds(6, L), :] + cw[0:1] * xp_ref[pl.ds(5, L), :])
    act = _silu(conv)
    xs = act[:, :SSM_INNER]
    bm = act[:, SSM_INNER:SSM_INNER + SSM_GN]
    cm = act[:, SSM_INNER + SSM_GN:]

    row = lax.broadcasted_iota(jnp.int32, (L, L), 0)
    col = lax.broadcasted_iota(jnp.int32, (L, L), 1)
    causal = row >= col
    tril = causal.astype(BF16)

    dt_s = _softplus(dt_ref[...] + dtbs_ref[...])
    a_cs_s = _dot3_left(tril, dt_s * as_ref[...])
    a_cs_t = a_cs_s.T
    dt = _dot3_right(dt_s, e_ref[...])
    a_cs = _dot3_right(a_cs_s, e_ref[...])

    xdt = xs * dt
    ea = jnp.exp(a_cs)
    a_last = a_cs[L - 1:L, :]
    cdec = jnp.exp(a_last)
    xdt_b = xdt.astype(BF16)
    xw_b = (xdt * jnp.exp(a_last - a_cs)).astype(BF16)

    for g in range(SSM_GROUPS):
        bg = bm[:, g * N:(g + 1) * N]
        cg_b = cm[:, g * N:(g + 1) * N].astype(BF16)
        cb = _dot_nt(cg_b, bg.astype(BF16))
        st = st_ref[g]
        y_off = _dot(cg_b, st.astype(BF16)) * ea[:, g * RP:(g + 1) * RP]
        for r in range(SSM_HEADS // SSM_GROUPS):
            h = g * (SSM_HEADS // SSM_GROUPS) + r
            seg = a_cs_s[:, h:h + 1] - a_cs_t[h:h + 1, :]
            dec = jnp.exp(jnp.where(causal, seg, NEG))
            y_d = _dot((cb * dec).astype(BF16), xdt_b[:, h * P:(h + 1) * P])
            y_ref[:, h * P:(h + 1) * P] = y_d + y_off[:, r * P:(r + 1) * P]
        new = _dot(bg.T.astype(BF16), xw_b[:, g * RP:(g + 1) * RP])
        st_ref[g] = st * cdec[:, g * RP:(g + 1) * RP] + new

    y = y_ref[...] + dsk_ref[...] * xs
    yg = y * _silu(za_ref[...])
    o_ref[...] = (yg * lax.rsqrt(jnp.mean(yg * yg, axis=-1, keepdims=True) + EPS) * nw_ref[...]).astype(o_ref.dtype)


def _ssd(proj, layout, conv_w, conv_b, dt_bias, a_log, d_skip, ssm_norm_w):
    s = proj.shape[0]
    L = SSM_CHUNK
    rep = lambda v: jnp.repeat(v.astype(F32), SSM_HEAD_DIM)[None, :]
    pad = lambda v: jnp.pad(v.astype(F32), (0, LANES - SSM_HEADS))[None, :]
    a = -jnp.exp(a_log.astype(F32))
    expand = np.zeros((LANES, SSM_INNER), np.float32)
    expand[np.arange(SSM_INNER) // SSM_HEAD_DIM, np.arange(SSM_INNER)] = 1.0
    const = lambda shape: pl.BlockSpec(shape, lambda c: (0,) * len(shape))
    return pl.pallas_call(
        _ssd_kernel,
        grid=(s // L,),
        in_specs=[pl.BlockSpec((L, SSM_XBC), lambda c: (c, _col_block(layout, "xbc", SSM_XBC))),
                  pl.BlockSpec((L, SSM_INNER), lambda c: (c, _col_block(layout, "z_a", SSM_INNER))),
                  pl.BlockSpec((L, LANES), lambda c: (c, _col_block(layout, "dt", LANES))),
                  const((SSM_CONV, SSM_XBC)), const((1, SSM_XBC)),
                  const((1, SSM_INNER)), const((1, SSM_INNER)),
                  const((1, LANES)), const((1, LANES)), const((LANES, SSM_INNER))],
        out_specs=pl.BlockSpec((L, SSM_INNER), lambda c: (c, 0)),
        out_shape=jax.ShapeDtypeStruct((s, SSM_INNER), BF16),
        scratch_shapes=[pltpu.VMEM((L + 8, SSM_XBC), F32),
                        pltpu.VMEM((SSM_GROUPS, SSM_STATE, SSM_RP), F32),
                        pltpu.VMEM((L, SSM_INNER), F32)],
        compiler_params=_params(("arbitrary",)),
        name="ssd",
    )(proj, proj, proj, conv_w, conv_b[None, :], rep(d_skip), ssm_norm_w[None, :],
      pad(dt_bias), pad(a), jnp.asarray(expand, BF16))


def _prep_kernel(q_ref, ks_ref, vsi_ref, kwi_ref, vwi_ref, cos_ref, sa_ref, sb_ref, qr_ref, qt_ref, ksa_ref, vt_ref,
                 kw_ref, vwt_ref, *, tm, slots):
    cosf, sa, sb = cos_ref[...], sa_ref[...], sb_ref[...]
    hd = NSA_HEAD_DIM

    def rope(t):
        return t * cosf + pltpu.roll(t, hd - ROPE_HALF, 1) * sa + pltpu.roll(t, ROPE_HALF, 1) * sb

    scale = hd ** -0.5 * LOG2E
    for h in range(NSA_HEADS):
        t = q_ref[:, h * hd:(h + 1) * hd]
        qr_ref[:, h * hd:(h + 1) * hd] = (t * scale).astype(BF16)
        qt_ref[h] = (rope(t) * scale).T.astype(BF16)

    pos = pl.program_id(0) * tm + lax.broadcasted_iota(jnp.int32, (tm, slots), 0)
    lane = lax.broadcasted_iota(jnp.int32, (tm, slots), 1)
    onehot = jnp.where((pos // SEL_LEN) % slots == lane, 1.0, 0.0).astype(BF16)
    for g in range(NSA_KV_GROUPS):
        cols = slice(g * hd, (g + 1) * hd)
        ksa_ref[g, :, :hd] = rope(ks_ref[:, cols]).astype(BF16)
        ksa_ref[g, :, hd:] = onehot
        vt_ref[g, 0, :hd, :] = vsi_ref[:, cols].T.astype(BF16)
        vt_ref[g, 0, hd:, :] = jnp.ones((SEL_V_ROWS - hd, tm), BF16)
        kw_ref[g] = rope(kwi_ref[:, cols]).astype(BF16)
        vwt_ref[g] = vwi_ref[:, cols].T.astype(BF16)


def _nsa_prep(proj, layout, cos, sin, slots, tk):
    s = proj.shape[0]
    tm = _tile(tk, 512)
    hd, G = NSA_HEAD_DIM, NSA_KV_GROUPS
    ones = jnp.ones((s, hd - ROPE_DIM), F32)
    zeros = lambda w: jnp.zeros((s, w), F32)
    cosf = jnp.concatenate([cos, cos, ones], axis=1)
    sa = jnp.concatenate([-sin, zeros(hd - ROPE_HALF)], axis=1)
    sb = jnp.concatenate([zeros(ROPE_HALF), sin, zeros(hd - ROPE_DIM)], axis=1)
    tab = pl.BlockSpec((tm, hd), lambda i: (i, 0))
    grp = lambda w: pl.BlockSpec((G, tm, w), lambda i: (0, i, 0))
    kvb = _col_block(layout, "kv", KV_W)
    return pl.pallas_call(
        functools.partial(_prep_kernel, tm=tm, slots=slots),
        grid=(s // tm,),
        in_specs=[pl.BlockSpec((tm, NSA_INNER), lambda i: (i, _col_block(layout, "q", NSA_INNER))),
                  *[pl.BlockSpec((tm, KV_W), functools.partial(lambda i, k: (i, k), k=kvb + k)) for k in (2, 3, 4, 5)],
                  tab, tab, tab],
        out_specs=[pl.BlockSpec((tm, NSA_INNER), lambda i: (i, 0)),
                   pl.BlockSpec((NSA_HEADS, hd, tm), lambda i: (0, 0, i)),
                   grp(hd + slots),
                   pl.BlockSpec((G, 1, SEL_V_ROWS, tm), lambda i: (0, i // (tk // tm), 0, i % (tk // tm))),
                   grp(hd),
                   pl.BlockSpec((G, hd, tm), lambda i: (0, 0, i))],
        out_shape=[jax.ShapeDtypeStruct((s, NSA_INNER), BF16), jax.ShapeDtypeStruct((NSA_HEADS, hd, s), BF16),
                   jax.ShapeDtypeStruct((G, s, hd + slots), BF16),
                   jax.ShapeDtypeStruct((G, s // tk, SEL_V_ROWS, tk), BF16),
                   jax.ShapeDtypeStruct((G, s, hd), BF16), jax.ShapeDtypeStruct((G, hd, s), BF16)],
        compiler_params=_params(("parallel",)),
        name="nsa_prep",
    )(proj, proj, proj, proj, proj, cosf, sa, sb)


def _cmp_kernel(x_ref, pe_ref, w1_ref, w2_ref, o_ref):
    x = x_ref[0, 0]
    nc1, half = x.shape
    a = _dot((x + pe_ref[0, 0:1]).astype(BF16), w1_ref[0, :half])
    b = _dot((x + pe_ref[0, 1:2]).astype(BF16), w1_ref[0, half:])
    pre = a + pltpu.roll(b, nc1 - 1, 0)
    hdn = 0.5 * pre * (1.0 + jnp.tanh(0.7978845608028654 * (pre + 0.044715 * pre * pre * pre)))
    o_ref[0, 0] = _dot(hdn.astype(BF16), w2_ref[0]).astype(o_ref.dtype)


def _nsa_cmp(xkv, pe, w1, w2):
    _, G, nc1, half = xkv.shape
    return pl.pallas_call(
        _cmp_kernel,
        grid=(2, G),
        in_specs=[pl.BlockSpec((1, 1, nc1, half), lambda t, g: (t, g, 0, 0)),
                  pl.BlockSpec((1, 2, half), lambda t, g: (t, 0, 0)),
                  pl.BlockSpec((1, 2 * half, CMP_HIDDEN), lambda t, g: (t, 0, 0)),
                  pl.BlockSpec((1, CMP_HIDDEN, NSA_HEAD_DIM), lambda t, g: (t, 0, 0))],
        out_specs=pl.BlockSpec((1, 1, nc1, NSA_HEAD_DIM), lambda t, g: (t, g, 0, 0)),
        out_shape=jax.ShapeDtypeStruct((2, G, nc1, NSA_HEAD_DIM), BF16),
        compiler_params=_params(("parallel", "parallel")),
        name="nsa_cmp",
    )(xkv, pe, w1, w2)


def _stack_heads(q):
    return jnp.concatenate([q[:, r * NSA_HEAD_DIM:(r + 1) * NSA_HEAD_DIM] for r in range(NSA_R)], axis=0)


def _attn_cmp_kernel(q_ref, kc_ref, vc_ref, gt_ref, ov_ref, oc_ref, sb_ref, imp_ref, *, ns, chunk):
    QB, R, hd, G = Q_BLOCK, NSA_R, NSA_HEAD_DIM, NSA_KV_GROUPS
    i = pl.program_id(0)
    t0 = i * QB
    nc1 = kc_ref.shape[2]
    gate = jax.nn.sigmoid(gt_ref[...])

    def branch(nk):
        pos = t0 + lax.broadcasted_iota(jnp.int32, (QB, nk), 0)
        kend = lax.broadcasted_iota(jnp.int32, (QB, nk), 1) * CMP_STRIDE + (CMP_LEN - 1)
        vis = (kend <= pos)[None]
        for g in range(G):
            q = q_ref[:, g * R * hd:(g + 1) * R * hd]
            sm = jnp.where(vis, _dot_nt(_stack_heads(q), kc_ref[0, g, :nk, :]).reshape(R, QB, nk), NEG)
            e = jnp.where(vis, jnp.exp2(sm - jnp.max(sm, axis=-1, keepdims=True)), 0.0)
            p = e / jnp.maximum(jnp.sum(e, axis=-1, keepdims=True), 1e-30)
            oc = _dot(p.reshape(R * QB, nk).astype(BF16), vc_ref[0, g, :nk, :])
            for r in range(R):
                h = g * R + r
                oc_ref[:, h * hd:(h + 1) * hd] = oc[r * QB:(r + 1) * QB] * gate[:, g * LANES + r:g * LANES + r + 1]
            imp_ref[g * QB:(g + 1) * QB, :] = _dot3_right(jnp.sum(p, axis=0), ov_ref[:nk, :])

    need = (i * (QB // CMP_STRIDE) + (QB - CMP_LEN) // CMP_STRIDE + chunk) // chunk
    for n in range(1, nc1 // chunk + 1):
        pl.when(need == n)(functools.partial(branch, n * chunk))

    shape = (ns, G * QB)
    imp_t = jnp.concatenate([imp_ref[g * QB:(g + 1) * QB, :].T for g in range(G)], axis=1)
    jb = lax.broadcasted_iota(jnp.int32, shape, 0)
    pq = t0 + (lax.broadcasted_iota(jnp.int32, shape, 1) & (QB - 1))
    cur = pq // SEL_LEN
    eligible = jb * SEL_LEN <= pq
    forced = (jb == 0) | (jb == cur) | (jb == cur - 1)
    score = jnp.where(forced, REMOVED, jnp.where(eligible, imp_t, -BIG))
    jbf = jb.astype(F32)

    def take(_, score):
        mx = jnp.max(score, axis=0, keepdims=True)
        idx = jnp.min(jnp.where(score == mx, jbf, float(ns)), axis=0, keepdims=True)
        return jnp.where(jbf == idx, REMOVED, score)

    score = lax.fori_loop(0, SEL_TOP - 3, take, score)
    bias = jnp.where((score == REMOVED) & eligible, 0.0, NEG).astype(BF16)
    nsp = sb_ref.shape[1]
    for g in range(G):
        sb_ref[g, :ns, :] = bias[:, g * QB:(g + 1) * QB]
        if nsp > ns:
            sb_ref[g, ns:, :] = jnp.full((nsp - ns, QB), NEG, BF16)


def _attn_cmp(q_raw, kvc, proj, layout, nsp):
    s = q_raw.shape[0]
    G, QB, hd = NSA_KV_GROUPS, Q_BLOCK, NSA_HEAD_DIM
    nc1 = kvc.shape[2]
    ns = s // SEL_LEN
    ic = np.arange(nc1)[:, None]
    jc = np.arange(ns)[None, :]
    overlap = ((ic * CMP_STRIDE < (jc + 1) * SEL_LEN) & (ic * CMP_STRIDE + CMP_LEN > jc * SEL_LEN)
               & (ic < nc1 - 1)).astype(np.float32)
    assert ns >= SEL_TOP
    chunk = _tile(nc1, 256)
    return pl.pallas_call(
        functools.partial(_attn_cmp_kernel, ns=ns, chunk=chunk),
        grid=(s // QB,),
        in_specs=[pl.BlockSpec((QB, NSA_INNER), lambda i: (i, 0)),
                  pl.BlockSpec((1, G, nc1, hd), lambda i: (0, 0, 0, 0)),
                  pl.BlockSpec((1, G, nc1, hd), lambda i: (1, 0, 0, 0)),
                  pl.BlockSpec((QB, G * LANES), lambda i: (i, _col_block(layout, "nsa", G * LANES))),
                  pl.BlockSpec((nc1, ns), lambda i: (0, 0))],
        out_specs=[pl.BlockSpec((QB, NSA_INNER), lambda i: (i, 0)),
                   pl.BlockSpec((G, nsp, QB), lambda i: (0, 0, i))],
        out_shape=[jax.ShapeDtypeStruct((s, NSA_INNER), F32), jax.ShapeDtypeStruct((G, nsp, s), BF16)],
        scratch_shapes=[pltpu.VMEM((G * QB, ns), F32)],
        compiler_params=_params(("parallel",)),
        name="attn_cmp",
    )(q_raw, kvc, kvc, proj, jnp.asarray(overlap, BF16))


def _attn_sel_kernel(qt_ref, sb_ref, ksa_ref, vt_ref, *rest, tk, slots, nwin, cb, look, unroll):
    kw_refs, vwt_refs = rest[:nwin], rest[nwin:2 * nwin]
    gt_ref, oc_ref, zb_ref, o_ref, qa_ref, m_ref, acc_ref = rest[2 * nwin:]
    QB, R, hd = Q_BLOCK, NSA_R, NSA_HEAD_DIM
    RQ = R * QB
    t0 = pl.program_id(1) * QB
    for hf in range(qa_ref.shape[0]):
        for r in range(R):
            qa_ref[hf, :hd, r * QB:(r + 1) * QB] = qt_ref[r]
            qa_ref[hf, hd:, r * QB:(r + 1) * QB] = sb_ref[0, hf * slots:(hf + 1) * slots, :]
    m_ref[...] = jnp.full(m_ref.shape, NEG, F32)
    acc_ref[...] = jnp.zeros(acc_ref.shape, F32)
    tiles_per_half = slots * SEL_LEN // tk

    ncb = RQ // cb

    def scores(kt, c):
        k0 = pl.multiple_of(kt * tk, tk)
        qa = qa_ref[kt // tiles_per_half, :, c * cb:(c + 1) * cb]
        return _dot(ksa_ref[0, pl.ds(k0, tk), :], qa)

    def accumulate(kt, c, s):
        cols = slice(c * cb, (c + 1) * cb)
        m_old = m_ref[:, cols]
        m_new = jnp.maximum(m_old, jnp.max(s, axis=0, keepdims=True))
        alpha = jnp.exp2(m_old - m_new)
        p = jnp.exp2(s - m_new).astype(BF16)
        m_ref[:, cols] = m_new
        acc_ref[:, cols] = alpha * acc_ref[:, cols] + _dot(vt_ref[0, kt], p)

    kt_last = (t0 + QB - 1) // tk

    def tiles(kt, pend, n):
        pend = list(pend)
        for t in range(n):
            for c in range(ncb):
                ahead = c + look
                pend.append(scores(kt + t, ahead) if ahead < ncb else scores(kt + t + 1, ahead - ncb))
                accumulate(kt + t, c, pend.pop(0))
        return tuple(pend)

    n_main = kt_last // unroll
    pend = lax.fori_loop(0, n_main, lambda u, p: tiles(u * unroll, p, unroll),
                         tuple(scores(0, c) for c in range(look)))
    pend = list(lax.fori_loop(n_main * unroll, kt_last, lambda kt, p: tiles(kt, p, 1), pend))
    qpos = t0 + (lax.broadcasted_iota(jnp.int32, (tk, cb), 1) & (QB - 1))
    causal = kt_last * tk + lax.broadcasted_iota(jnp.int32, (tk, cb), 0) <= qpos
    for c in range(ncb):
        if c + look < ncb:
            pend.append(scores(kt_last, c + look))
        accumulate(kt_last, c, jnp.where(causal, pend.pop(0), NEG))

    kw = jnp.concatenate([r[0] for r in kw_refs], axis=0)
    vwt = jnp.concatenate([r[0] for r in vwt_refs], axis=1)
    nk = nwin * QB
    kpos = t0 - WINDOW + lax.broadcasted_iota(jnp.int32, (nk, cb), 0)
    dlt = t0 + (lax.broadcasted_iota(jnp.int32, (nk, cb), 1) & (QB - 1)) - kpos
    vis = (dlt >= 0) & (dlt < WINDOW) & (kpos >= 0)
    gate = jax.nn.sigmoid(gt_ref[...])
    wscores = lambda c: _dot(kw, qa_ref[0, :hd, c * cb:(c + 1) * cb])
    pend = [wscores(c) for c in range(min(look, ncb))]
    for c in range(ncb):
        if c + look < ncb:
            pend.append(wscores(c + look))
        sw = jnp.where(vis, pend.pop(0), NEG)
        ew = jnp.where(vis, jnp.exp2(sw - jnp.max(sw, axis=0, keepdims=True)), 0.0)
        owt = _dot(vwt, ew.astype(BF16)) / jnp.maximum(jnp.sum(ew, axis=0, keepdims=True), 1e-30)
        ost = acc_ref[:hd, c * cb:(c + 1) * cb] / jnp.maximum(acc_ref[hd:hd + 1, c * cb:(c + 1) * cb], 1e-30)
        for j in range(cb // QB):
            r = c * (cb // QB) + j
            cols = slice(r * hd, (r + 1) * hd)
            lanes = slice(j * QB, (j + 1) * QB)
            o = (oc_ref[:, cols] + ost[:, lanes].T * gate[:, R + r:R + r + 1]
                 + owt[:, lanes].T * gate[:, 2 * R + r:2 * R + r + 1])
            o_ref[:, cols] = (o * _silu(zb_ref[:, cols])).astype(o_ref.dtype)


def _attn_sel(q_rot_t, selb, ksa, vt, kw, vwt, proj, layout, oc, slots, tk):
    G, QB, hd, R = NSA_KV_GROUPS, Q_BLOCK, NSA_HEAD_DIM, NSA_R
    s = ksa.shape[1]
    nsp = selb.shape[1]
    nwin = WINDOW // QB + 1
    wblk =lambda j: (lambda g, i: (g, jnp.maximum(i - (nwin - 1) + j, 0), 0))
    wblk_t = lambda j: (lambda g, i: (g, 0, jnp.maximum(i - (nwin - 1) + j, 0)))
    oblk = pl.BlockSpec((QB, R * hd), lambda g, i: (i, g))
    return pl.pallas_call(
        functools.partial(_attn_sel_kernel, tk=tk, slots=slots, nwin=nwin, cb=SEL_COL_BLOCK, look=SEL_LOOKAHEAD,
                          unroll=SEL_UNROLL),
        grid=(G, s // QB),
        in_specs=[pl.BlockSpec((R, hd, QB), lambda g, i: (g, 0, i)),
                  pl.BlockSpec((1, nsp, QB), lambda g, i: (g, 0, i)),
                  pl.BlockSpec((1, s, hd + slots), lambda g, i: (g, 0, 0)),
                  pl.BlockSpec((1, s // tk, SEL_V_ROWS, tk), lambda g, i: (g, 0, 0, 0)),
                  *[pl.BlockSpec((1, QB, hd), wblk(j)) for j in range(nwin)],
                  *[pl.BlockSpec((1, hd, QB), wblk_t(j)) for j in range(nwin)],
                  pl.BlockSpec((QB, LANES), lambda g, i: (i, _col_block(layout, "nsa", LANES) + g)),
                  oblk,
                  pl.BlockSpec((QB, R * hd), lambda g, i: (i, _col_block(layout, "z_b", R * hd) + g))],
        out_specs=oblk,
        out_shape=jax.ShapeDtypeStruct((s, NSA_INNER), BF16),
        scratch_shapes=[pltpu.VMEM((nsp // slots, hd + slots, R * QB), BF16),
                        pltpu.VMEM((1, R * QB), F32), pltpu.VMEM((SEL_V_ROWS, R * QB), F32)],
        compiler_params=_params(("parallel", "arbitrary")),
        name="attn_sel",
    )(q_rot_t, selb, ksa, vt, *([kw] * nwin), *([vwt] * nwin), proj, oc, proj)


def _merge_kernel(ya_ref, ob_ref, wa_ref, wb_ref, ga_ref, gb_ref, m_ref):
    u_a = _dot(ya_ref[...], wa_ref[...])
    u_b = _dot(ob_ref[...], wb_ref[...])
    m_ref[...] = (jax.nn.sigmoid(ga_ref[...]) * u_a + jax.nn.sigmoid(gb_ref[...]) * u_b).astype(m_ref.dtype)


def _merge(y_a, ob, proj, layout, wa, wb):
    s = y_a.shape[0]
    d = wa.shape[1]
    tm, tn = _tile(s, 1024), _tile(d, 512)
    return pl.pallas_call(
        _merge_kernel,
        grid=(s // tm, d // tn),
        in_specs=[pl.BlockSpec((tm, SSM_INNER), lambda i, j: (i, 0)),
                  pl.BlockSpec((tm, NSA_INNER), lambda i, j: (i, 0)),
                  pl.BlockSpec((SSM_INNER, tn), lambda i, j: (0, j)),
                  pl.BlockSpec((NSA_INNER, tn), lambda i, j: (0, j)),
                  pl.BlockSpec((tm, tn), lambda i, j: (i, _col_block(layout, "g_a", tn) + j)),
                  pl.BlockSpec((tm, tn), lambda i, j: (i, _col_block(layout, "g_b", tn) + j))],
        out_specs=pl.BlockSpec((tm, tn), lambda i, j: (i, j)),
        out_shape=jax.ShapeDtypeStruct((s, d), BF16),
        compiler_params=_params(("parallel", "arbitrary")),
        name="merge",
    )(y_a, ob, wa, wb, proj, proj)


def _out_kernel(m_ref, w_ref, x_ref, gate_ref, fnw_ref, o_ref, r_ref, *, nt):
    j = pl.program_id(1)
    r_ref[j] = x_ref[...] + gate_ref[...] * _dot(m_ref[...], w_ref[...])

    @pl.when(j == nt - 1)
    def _():
        tn = r_ref.shape[2]
        ss = sum(jnp.sum(r_ref[t] * r_ref[t], axis=-1, keepdims=True) for t in range(nt))
        inv = lax.rsqrt(ss / (nt * tn) + EPS)
        for t in range(nt):
            o_ref[:, t * tn:(t + 1) * tn] = r_ref[t] * inv * fnw_ref[:, t * tn:(t + 1) * tn]


def _out_proj(merged, w_out, x2, mod, fnw):
    s, d = x2.shape
    tm, tn = min(512, s), min(512, d)
    nt = d // tn
    return pl.pallas_call(
        functools.partial(_out_kernel, nt=nt),
        grid=(s // tm, nt),
        in_specs=[pl.BlockSpec((tm, d), lambda i, j: (i, 0)),
                  pl.BlockSpec((d, tn), lambda i, j: (0, j)),
                  pl.BlockSpec((tm, tn), lambda i, j: (i, j)),
                  pl.BlockSpec((1, tn), lambda i, j: (0, 2 * nt + j)),
                  pl.BlockSpec((1, d), lambda i, j: (0, 0))],
        out_specs=pl.BlockSpec((tm, d), lambda i, j: (i, 0)),
        out_shape=jax.ShapeDtypeStruct((s, d), F32),
        scratch_shapes=[pltpu.VMEM((nt, tm, tn), F32)],
        compiler_params=_params(("parallel", "arbitrary")),
        name="out_proj",
    )(merged, w_out, x2, mod, fnw)


def _layer(x2, c, w_ada, b_ada, norm_w, w_in, conv_w, conv_b, dt_bias, a_log, d_skip, ssm_norm_w, cmp_pe_k, cmp_pe_v,
           cmp_k_w1, cmp_k_w2, cmp_v_w1, cmp_v_w2, w_proj_a, w_proj_b, w_out, final_norm_w, cos, sin):
    s, d = x2.shape
    G, hd = NSA_KV_GROUPS, NSA_HEAD_DIM
    layout, width = _proj_layout(d)
    n_pad = -(-width // 512) * 512
    slots = SEL_SLOTS
    nsp = max(s // SEL_LEN, slots)
    assert nsp % slots == 0 and (slots * SEL_LEN) % SEL_KEY_TILE == 0 and s % SEL_KEY_TILE == 0

    mod = _ada_mod(c.reshape(d, 1), w_ada, b_ada[None, :])
    proj = _in_proj(_norm_mod(x2, norm_w[None, :], mod), _build_w_in(w_in, d, n_pad))

    y_a = _ssd(proj, layout, conv_w, conv_b, dt_bias, a_log, d_skip, ssm_norm_w)

    q_raw, q_rot_t, ksa, vt, kw, vwt = _nsa_prep(proj, layout, cos, sin, slots, SEL_KEY_TILE)
    kv0 = layout["kv"]
    half_blocks = lambda t: jnp.transpose(t.reshape(s // CMP_STRIDE, CMP_STRIDE, G, hd), (2, 0, 1, 3)).reshape(
        G, s // CMP_STRIDE, CMP_STRIDE * hd)
    xkv = jnp.stack([half_blocks(proj[:, kv0:kv0 + KV_W]), half_blocks(proj[:, kv0 + KV_W:kv0 + 2 * KV_W])])
    pe = jnp.stack([cmp_pe_k.reshape(2, CMP_STRIDE * hd), cmp_pe_v.reshape(2, CMP_STRIDE * hd)])
    kvc = _nsa_cmp(xkv, pe, jnp.stack([cmp_k_w1, cmp_v_w1]).astype(BF16), jnp.stack([cmp_k_w2, cmp_v_w2]).astype(BF16))

    oc, selb = _attn_cmp(q_raw, kvc, proj, layout, nsp)
    o = _attn_sel(q_rot_t, selb, ksa, vt, kw, vwt, proj, layout, oc, slots, SEL_KEY_TILE)

    merged = _merge(y_a, o, proj, layout, w_proj_a.astype(BF16), w_proj_b.astype(BF16))
    return _out_proj(merged, w_out.astype(BF16), x2, mod, final_norm_w[None, :])


def kernel(x, c, w_ada, b_ada, norm_w, w_in, conv_w, conv_b, dt_bias, a_log, d_skip, ssm_norm_w, cmp_pe_k, cmp_pe_v,
           cmp_k_w1, cmp_k_w2, cmp_v_w1, cmp_v_w2, w_proj_a, w_proj_b, w_out, final_norm_w):
    b, s, d = x.shape
    assert b == 1 and w_ada.shape[0] == 1, "one sequence, one layer"
    pos = jnp.arange(s, dtype=F32)
    inv_freq = ROPE_THETA ** (-jnp.arange(0, ROPE_DIM, 2, dtype=F32) / ROPE_DIM)
    ang = pos[:, None] * inv_freq[None, :]
    out = _layer(x[0], c, w_ada[0], b_ada[0], norm_w[0], w_in[0], conv_w[0], conv_b[0], dt_bias[0], a_log[0],
                 d_skip[0], ssm_norm_w[0], cmp_pe_k[0], cmp_pe_v[0], cmp_k_w1[0], cmp_k_w2[0], cmp_v_w1[0],
                 cmp_v_w2[0], w_proj_a[0], w_proj_b[0], w_out[0], final_norm_w, jnp.cos(ang), jnp.sin(ang))
    return out[None]
```

```python
import functools

import numpy as np
import jax
import jax.numpy as jnp
from jax import lax
from jax.experimental import pallas as pl
from jax.experimental.pallas import tpu as pltpu

F32 = jnp.float32
BF16 = jnp.bfloat16

EPS = 1e-6
BIG = 1e30
NEG = -1e30
REMOVED = -3e38

SSM_HEADS = 32
SSM_HEAD_DIM = 64
SSM_INNER = SSM_HEADS * SSM_HEAD_DIM
SSM_GROUPS = 8
SSM_STATE = 128
SSM_CONV = 4
SSM_CHUNK = 128
SSM_GN = SSM_GROUPS * SSM_STATE
SSM_XBC = SSM_INNER + 2 * SSM_GN
SSM_RP = SSM_INNER // SSM_GROUPS

NSA_HEADS = 16
NSA_KV_GROUPS = 2
NSA_R = NSA_HEADS // NSA_KV_GROUPS
NSA_HEAD_DIM = 128
NSA_INNER = NSA_HEADS * NSA_HEAD_DIM
KV_W = NSA_KV_GROUPS * NSA_HEAD_DIM
CMP_STRIDE = 16
CMP_LEN = 32
CMP_HIDDEN = 256
SEL_LEN = 64
SEL_TOP = 16
WINDOW = 512
Q_BLOCK = 128
ROPE_THETA = 500000.0
ROPE_DIM = NSA_HEAD_DIM // 4
ROPE_HALF = ROPE_DIM // 2

LANES = 128
SEL_SLOTS = 128
SEL_KEY_TILE = 512
SEL_COL_BLOCK = 256
SEL_LOOKAHEAD = 2
SEL_UNROLL = (8, 2, 1)
SEL_V_ROWS = NSA_HEAD_DIM + 16
LOG2E = 1.4426950408889634
VMEM_LIMIT = 56 * 1024 * 1024


def _params(sem, vmem=VMEM_LIMIT):
    return pltpu.CompilerParams(dimension_semantics=sem, vmem_limit_bytes=vmem)


def _tile(n, pref=512):
    t = min(pref, n)
    while n % t:
        t //= 2
    return t


def _col_block(layout, name, width):
    assert layout[name] % width == 0, (name, layout[name], width)
    return layout[name] // width


def _silu(v):
    return v * jax.nn.sigmoid(v)


def _softplus(v):
    return jnp.maximum(v, 0.0) + jnp.log1p(jnp.exp(-jnp.abs(v)))


def _split3(v):
    hi = v.astype(BF16)
    r = v - hi.astype(F32)
    mid = r.astype(BF16)
    lo = (r - mid.astype(F32)).astype(BF16)
    return hi, mid, lo


def _dot(a, b):
    return jnp.dot(a, b, preferred_element_type=F32)


def _dot_nt(a, b):
    return lax.dot_general(a, b, (((1,), (1,)), ((), ())), preferred_element_type=F32)


def _dot3_left(m01, v):
    hi, mid, lo = _split3(v)
    return _dot(m01, hi) + _dot(m01, mid) + _dot(m01, lo)


def _dot3_right(v, m01):
    hi, mid, lo = _split3(v)
    return _dot(hi, m01) + _dot(mid, m01) + _dot(lo, m01)


def _proj_layout(d_model):
    segs = [("xbc", SSM_XBC), ("g_a", d_model), ("g_b", d_model), ("z_a", SSM_INNER), ("q", NSA_INNER),
            ("z_b", NSA_INNER), ("kv", 6 * KV_W), ("nsa", NSA_KV_GROUPS * LANES), ("dt", LANES)]
    segs = sorted(segs, key=lambda s: -s[1])
    off, out = 0, {}
    for name, w in segs:
        out[name] = off
        off += w
    return out, off


def _src_offsets(d_model):
    splits = (SSM_INNER, SSM_XBC, SSM_HEADS, NSA_INNER, KV_W, KV_W, KV_W, KV_W, KV_W, KV_W,
              NSA_HEADS * 3, NSA_INNER, d_model, d_model)
    names = ("z_a", "xbc", "dt", "q", "kc", "vc", "ks", "vs", "kw", "vw", "nsa", "z_b", "g_a", "g_b")
    offs = np.concatenate([[0], np.cumsum(splits)])
    return {n: (int(offs[i]), int(splits[i])) for i, n in enumerate(names)}


def _build_w_in(w_in, d_model, n_pad):
    src = _src_offsets(d_model)
    layout, width = _proj_layout(d_model)

    def cols(name, w=None):
        a, n = src[name]
        return w_in[:, a:a + (n if w is None else w)]

    nsa = cols("nsa").reshape(d_model, NSA_KV_GROUPS, NSA_R, 3)
    nsa = jnp.transpose(nsa, (0, 1, 3, 2)).reshape(d_model, NSA_KV_GROUPS, 3 * NSA_R)
    nsa = jnp.pad(nsa, ((0, 0), (0, 0), (0, LANES - 3 * NSA_R))).reshape(d_model, NSA_KV_GROUPS * LANES)
    pieces = {
        "xbc": cols("xbc"), "g_a": cols("g_a"), "g_b": cols("g_b"), "z_a": cols("z_a"), "q": cols("q"),
        "z_b": cols("z_b"), "kv": cols("kc", 6 * KV_W), "nsa": nsa,
        "dt": jnp.pad(cols("dt"), ((0, 0), (0, LANES - SSM_HEADS))),
    }
    order = sorted(layout, key=lambda n: layout[n])
    parts = [pieces[n].astype(BF16) for n in order]
    if n_pad > width:
        parts.append(jnp.zeros((d_model, n_pad - width), BF16))
    return jnp.concatenate(parts, axis=1)


def _mod_kernel(c_ref, w_ref, b_ref, o_ref):
    sc = _silu(c_ref[...])
    o_ref[...] = jnp.sum(w_ref[...] * sc, axis=0, keepdims=True) + b_ref[...]


def _ada_mod(c_col, w_ada, b_ada):
    d, n = w_ada.shape
    tn = _tile(n)
    return pl.pallas_call(
        _mod_kernel,
        grid=(n // tn,),
        in_specs=[pl.BlockSpec((d, 1), lambda j: (0, 0)),
                  pl.BlockSpec((d, tn), lambda j: (0, j)),
                  pl.BlockSpec((1, tn), lambda j: (0, j))],
        out_specs=pl.BlockSpec((1, tn), lambda j: (0, j)),
        out_shape=jax.ShapeDtypeStruct((1, n), F32),
        compiler_params=_params(("arbitrary",)),
        name="ada_mod",
    )(c_col, w_ada, b_ada)


def _norm_mod_kernel(x_ref, nw_ref, shift_ref, scl_ref, h_ref):
    x = x_ref[...]
    y = x * lax.rsqrt(jnp.mean(x * x, axis=-1, keepdims=True) + EPS) * nw_ref[...]
    h_ref[...] = (y * (1.0 + scl_ref[...]) + shift_ref[...]).astype(BF16)


def _norm_mod(x2, norm_w, mod):
    s, d = x2.shape
    tm = _tile(s, 256)
    return pl.pallas_call(
        _norm_mod_kernel,
        grid=(s // tm,),
        in_specs=[pl.BlockSpec((tm, d), lambda i: (i, 0)),
                  pl.BlockSpec((1, d), lambda i: (0, 0)),
                  pl.BlockSpec((1, d), lambda i: (0, 0)),
                  pl.BlockSpec((1, d), lambda i: (0, 1))],
        out_specs=pl.BlockSpec((tm, d), lambda i: (i, 0)),
        out_shape=jax.ShapeDtypeStruct((s, d), BF16),
        compiler_params=_params(("parallel",)),
        name="norm_mod",
    )(x2, norm_w, mod, mod)


def _in_proj_kernel(h_ref, w_ref, o_ref):
    o_ref[...] = _dot(h_ref[...], w_ref[...])


def _in_proj(h, w_in_p):
    s, d = h.shape
    n = w_in_p.shape[1]
    tm, tn = _tile(s, 1024), _tile(n, 1024)
    return pl.pallas_call(
        _in_proj_kernel,
        grid=(s // tm, n // tn),
        in_specs=[pl.BlockSpec((tm, d), lambda i, j: (i, 0)),
                  pl.BlockSpec((d, tn), lambda i, j: (0, j))],
        out_specs=pl.BlockSpec((tm, tn), lambda i, j: (i, j)),
        out_shape=jax.ShapeDtypeStruct((s, n), F32),
        compiler_params=_params(("parallel", "arbitrary")),
        name="in_proj",
    )(h, w_in_p)


def _ssd_kernel(xbc_ref, za_ref, dt_ref, cw_ref, cb_ref, dsk_ref, nw_ref, dtbs_ref, as_ref,
                e_ref, o_ref, xp_ref, st_ref, y_ref):
    L, N, P, RP = SSM_CHUNK, SSM_STATE, SSM_HEAD_DIM, SSM_RP

    @pl.when(pl.program_id(0) == 0)
    def _():
        st_ref[...] = jnp.zeros_like(st_ref)
        xp_ref[pl.ds(L, 8), :] = jnp.zeros((8, SSM_XBC), F32)

    xp_ref[pl.ds(0, 8), :] = xp_ref[pl.ds(L, 8), :]
    x = xbc_ref[...]
    xp_ref[pl.ds(8, L), :] = x
    cw = cw_ref[...]
    conv = (cb_ref[...] + cw[3:4] * x + cw[2:3] * xp_ref[pl.ds(7, L), :]
            + cw[1:2] * xp_ref[pl.ds(6, L), :] + cw[0:1] * xp_ref[pl.ds(5, L), :])
    act = _silu(conv)
    xs = act[:, :SSM_INNER]
    bm = act[:, SSM_INNER:SSM_INNER + SSM_GN]
    cm = act[:, SSM_INNER + SSM_GN:]

    row = lax.broadcasted_iota(jnp.int32, (L, L), 0)
    col = lax.broadcasted_iota(jnp.int32, (L, L), 1)
    causal = row >= col
    tril = causal.astype(BF16)

    dt_s = _softplus(dt_ref[...] + dtbs_ref[...])
    a_cs_s = _dot3_left(tril, dt_s * as_ref[...])
    a_cs_t = a_cs_s.T
    dt = _dot3_right(dt_s, e_ref[...])
    a_cs = _dot3_right(a_cs_s, e_ref[...])

    xdt = xs * dt
    ea = jnp.exp(a_cs)
    a_last = a_cs[L - 1:L, :]
    cdec = jnp.exp(a_last)
    xdt_b = xdt.astype(BF16)
    xw_b = (xdt * jnp.exp(a_last - a_cs)).astype(BF16)

    for g in range(SSM_GROUPS):
        bg = bm[:, g * N:(g + 1) * N]
        cg_b = cm[:, g * N:(g + 1) * N].astype(BF16)
        cb = _dot_nt(cg_b, bg.astype(BF16))
        st = st_ref[g]
        y_off = _dot(cg_b, st.astype(BF16)) * ea[:, g * RP:(g + 1) * RP]
        for r in range(SSM_HEADS // SSM_GROUPS):
            h = g * (SSM_HEADS // SSM_GROUPS) + r
            seg = a_cs_s[:, h:h + 1] - a_cs_t[h:h + 1, :]
            dec = jnp.exp(jnp.where(causal, seg, NEG))
            y_d = _dot((cb * dec).astype(BF16), xdt_b[:, h * P:(h + 1) * P])
            y_ref[:, h * P:(h + 1) * P] = y_d + y_off[:, r * P:(r + 1) * P]
        new = _dot(bg.T.astype(BF16), xw_b[:, g * RP:(g + 1) * RP])
        st_ref[g] = st * cdec[:, g * RP:(g + 1) * RP] + new

    y = y_ref[...] + dsk_ref[...] * xs
    yg = y * _silu(za_ref[...])
    o_ref[...] = (yg * lax.rsqrt(jnp.mean(yg * yg, axis=-1, keepdims=True) + EPS) * nw_ref[...]).astype(o_ref.dtype)


def _ssd(proj, layout, conv_w, conv_b, dt_bias, a_log, d_skip, ssm_norm_w):
    s = proj.shape[0]
    L = SSM_CHUNK
    rep = lambda v: jnp.repeat(v.astype(F32), SSM_HEAD_DIM)[None, :]
    pad = lambda v: jnp.pad(v.astype(F32), (0, LANES - SSM_HEADS))[None, :]
    a = -jnp.exp(a_log.astype(F32))
    expand = np.zeros((LANES, SSM_INNER), np.float32)
    expand[np.arange(SSM_INNER) // SSM_HEAD_DIM, np.arange(SSM_INNER)] = 1.0
    const = lambda shape: pl.BlockSpec(shape, lambda c: (0,) * len(shape))
    return pl.pallas_call(
        _ssd_kernel,
        grid=(s // L,),
        in_specs=[pl.BlockSpec((L, SSM_XBC), lambda c: (c, _col_block(layout, "xbc", SSM_XBC))),
                  pl.BlockSpec((L, SSM_INNER), lambda c: (c, _col_block(layout, "z_a", SSM_INNER))),
                  pl.BlockSpec((L, LANES), lambda c: (c, _col_block(layout, "dt", LANES))),
                  const((SSM_CONV, SSM_XBC)), const((1, SSM_XBC)),
                  const((1, SSM_INNER)), const((1, SSM_INNER)),
                  const((1, LANES)), const((1, LANES)), const((LANES, SSM_INNER))],
        out_specs=pl.BlockSpec((L, SSM_INNER), lambda c: (c, 0)),
        out_shape=jax.ShapeDtypeStruct((s, SSM_INNER), BF16),
        scratch_shapes=[pltpu.VMEM((L + 8, SSM_XBC), F32),
                        pltpu.VMEM((SSM_GROUPS, SSM_STATE, SSM_RP), F32),
                        pltpu.VMEM((L, SSM_INNER), F32)],
        compiler_params=_params(("arbitrary",)),
        name="ssd",
    )(proj, proj, proj, conv_w, conv_b[None, :], rep(d_skip), ssm_norm_w[None, :],
      pad(dt_bias), pad(a), jnp.asarray(expand, BF16))


def _prep_kernel(q_ref, ks_ref, vsi_ref, kwi_ref, vwi_ref, cos_ref, sa_ref, sb_ref, qr_ref, qt_ref, ksa_ref, vt_ref,
                 kw_ref, vwt_ref, *, tm, slots):
    cosf, sa, sb = cos_ref[...], sa_ref[...], sb_ref[...]
    hd = NSA_HEAD_DIM

    def rope(t):
        return t * cosf + pltpu.roll(t, hd - ROPE_HALF, 1) * sa + pltpu.roll(t, ROPE_HALF, 1) * sb

    scale = hd ** -0.5 * LOG2E
    for h in range(NSA_HEADS):
        t = q_ref[:, h * hd:(h + 1) * hd]
        qr_ref[h] = (t * scale).T.astype(BF16)
        qt_ref[h] = (rope(t) * scale).T.astype(BF16)

    pos = pl.program_id(0) * tm + lax.broadcasted_iota(jnp.int32, (tm, slots), 0)
    lane = lax.broadcasted_iota(jnp.int32, (tm, slots), 1)
    onehot = jnp.where((pos // SEL_LEN) % slots == lane, 1.0, 0.0).astype(BF16)
    for g in range(NSA_KV_GROUPS):
        cols = slice(g * hd, (g + 1) * hd)
        ksa_ref[g, :, :hd] = rope(ks_ref[:, cols]).astype(BF16)
        ksa_ref[g, :, hd:] = onehot
        vt_ref[g, 0, :hd, :] = vsi_ref[:, cols].T.astype(BF16)
        vt_ref[g, 0, hd:, :] = jnp.ones((SEL_V_ROWS - hd, tm), BF16)
        kw_ref[g] = rope(kwi_ref[:, cols]).astype(BF16)
        vwt_ref[g] = vwi_ref[:, cols].T.astype(BF16)


def _nsa_prep(proj, layout, cos, sin, slots, tk):
    s = proj.shape[0]
    tm = _tile(tk, 512)
    hd, G = NSA_HEAD_DIM, NSA_KV_GROUPS
    ones = jnp.ones((s, hd - ROPE_DIM), F32)
    zeros = lambda w: jnp.zeros((s, w), F32)
    cosf = jnp.concatenate([cos, cos, ones], axis=1)
    sa = jnp.concatenate([-sin, zeros(hd - ROPE_HALF)], axis=1)
    sb = jnp.concatenate([zeros(ROPE_HALF), sin, zeros(hd - ROPE_DIM)], axis=1)
    tab = pl.BlockSpec((tm, hd), lambda i: (i, 0))
    grp = lambda w: pl.BlockSpec((G, tm, w), lambda i: (0, i, 0))
    kvb = _col_block(layout, "kv", KV_W)
    return pl.pallas_call(
        functools.partial(_prep_kernel, tm=tm, slots=slots),
        grid=(s // tm,),
        in_specs=[pl.BlockSpec((tm, NSA_INNER), lambda i: (i, _col_block(layout, "q", NSA_INNER))),
                  *[pl.BlockSpec((tm, KV_W), functools.partial(lambda i, k: (i, k), k=kvb + k)) for k in (2, 3, 4, 5)],
                  tab, tab, tab],
        out_specs=[pl.BlockSpec((NSA_HEADS, hd, tm), lambda i: (0, 0, i)),
                   pl.BlockSpec((NSA_HEADS, hd, tm), lambda i: (0, 0, i)),
                   grp(hd + slots),
                   pl.BlockSpec((G, 1, SEL_V_ROWS, tm), lambda i: (0, i // (tk // tm), 0, i % (tk // tm))),
                   grp(hd),
                   pl.BlockSpec((G, hd, tm), lambda i: (0, 0, i))],
        out_shape=[jax.ShapeDtypeStruct((NSA_HEADS, hd, s), BF16), jax.ShapeDtypeStruct((NSA_HEADS, hd, s), BF16),
                   jax.ShapeDtypeStruct((G, s, hd + slots), BF16),
                   jax.ShapeDtypeStruct((G, s // tk, SEL_V_ROWS, tk), BF16),
                   jax.ShapeDtypeStruct((G, s, hd), BF16), jax.ShapeDtypeStruct((G, hd, s), BF16)],
        compiler_params=_params(("parallel",)),
        name="nsa_prep",
    )(proj, proj, proj, proj, proj, cosf, sa, sb)


def _cmp_kernel(x_ref, pe_ref, w1_ref, w2_ref, o_ref, ot_ref):
    x = x_ref[0, 0]
    nc1, half = x.shape
    a = _dot((x + pe_ref[0, 0:1]).astype(BF16), w1_ref[0, :half])
    b = _dot((x + pe_ref[0, 1:2]).astype(BF16), w1_ref[0, half:])
    pre = a + pltpu.roll(b, nc1 - 1, 0)
    hdn = 0.5 * pre * (1.0 + jnp.tanh(0.7978845608028654 * (pre + 0.044715 * pre * pre * pre)))
    out = _dot(hdn.astype(BF16), w2_ref[0])
    o_ref[0, 0] = out.astype(o_ref.dtype)
    ot_ref[0, 0] = out.T.astype(ot_ref.dtype)


def _nsa_cmp(xkv, pe, w1, w2):
    _, G, nc1, half = xkv.shape
    return pl.pallas_call(
        _cmp_kernel,
        grid=(2, G),
        in_specs=[pl.BlockSpec((1, 1, nc1, half), lambda t, g: (t, g, 0, 0)),
                  pl.BlockSpec((1, 2, half), lambda t, g: (t, 0, 0)),
                  pl.BlockSpec((1, 2 * half, CMP_HIDDEN), lambda t, g: (t, 0, 0)),
                  pl.BlockSpec((1, CMP_HIDDEN, NSA_HEAD_DIM), lambda t, g: (t, 0, 0))],
        out_specs=[pl.BlockSpec((1, 1, nc1, NSA_HEAD_DIM), lambda t, g: (t, g, 0, 0)),
                   pl.BlockSpec((1, 1, NSA_HEAD_DIM, nc1), lambda t, g: (t, g, 0, 0))],
        out_shape=[jax.ShapeDtypeStruct((2, G, nc1, NSA_HEAD_DIM), BF16),
                   jax.ShapeDtypeStruct((2, G, NSA_HEAD_DIM, nc1), BF16)],
        compiler_params=_params(("parallel", "parallel")),
        name="nsa_cmp",
    )(xkv, pe, w1, w2)


def _attn_cmp_kernel(qt_ref, kc_ref, vct_ref, gt_ref, ovt_ref, oc_ref, sb_ref, imp_ref, *, ns, chunk):
    QB, R, hd, G = Q_BLOCK, NSA_R, NSA_HEAD_DIM, NSA_KV_GROUPS
    i = pl.program_id(0)
    t0 = i * QB
    nc1 = kc_ref.shape[2]
    gate = jax.nn.sigmoid(gt_ref[...])

    def branch(nk):
        pos = t0 + (lax.broadcasted_iota(jnp.int32, (nk, R * QB), 1) & (QB - 1))
        kend = lax.broadcasted_iota(jnp.int32, (nk, R * QB), 0) * CMP_STRIDE + (CMP_LEN - 1)
        vis = kend <= pos
        for g in range(G):
            qt = jnp.concatenate([qt_ref[g * R + r] for r in range(R)], axis=1)
            sm = jnp.where(vis, _dot(kc_ref[0, g, :nk, :], qt), NEG)
            e = jnp.where(vis, jnp.exp2(sm - jnp.max(sm, axis=0, keepdims=True)), 0.0)
            p = e / jnp.maximum(jnp.sum(e, axis=0, keepdims=True), 1e-30)
            oct = _dot(vct_ref[0, g, :, :nk], p.astype(BF16))
            psum = p[:, :QB]
            for r in range(R):
                h = g * R + r
                oc_ref[:, h * hd:(h + 1) * hd] = oct[:, r * QB:(r + 1) * QB].T * gate[:, g * LANES + r:g * LANES + r + 1]
                if r:
                    psum = psum + p[:, r * QB:(r + 1) * QB]
            imp_ref[:, g * QB:(g + 1) * QB] = _dot3_left(ovt_ref[:, :nk], psum)

    need = (i * (QB // CMP_STRIDE) + (QB - CMP_LEN) // CMP_STRIDE + chunk) // chunk
    for n in range(1, nc1 // chunk + 1):
        pl.when(need == n)(functools.partial(branch, n * chunk))

    shape = (ns, G * QB)
    imp_t = imp_ref[...]
    jb = lax.broadcasted_iota(jnp.int32, shape, 0)
    pq = t0 + (lax.broadcasted_iota(jnp.int32, shape, 1) & (QB - 1))
    cur = pq // SEL_LEN
    eligible = jb * SEL_LEN <= pq
    forced = (jb == 0) | (jb == cur) | (jb == cur - 1)
    score = jnp.where(forced, REMOVED, jnp.where(eligible, imp_t, -BIG))
    jbf = jb.astype(F32)

    def take(_, score):
        mx = jnp.max(score, axis=0, keepdims=True)
        idx = jnp.min(jnp.where(score == mx, jbf, float(ns)), axis=0, keepdims=True)
        return jnp.where(jbf == idx, REMOVED, score)

    score = lax.fori_loop(0, SEL_TOP - 3, take, score)
    bias = jnp.where((score == REMOVED) & eligible, 0.0, NEG).astype(BF16)
    nsp = sb_ref.shape[1]
    for g in range(G):
        sb_ref[g, :ns, :] = bias[:, g * QB:(g + 1) * QB]
        if nsp > ns:
            sb_ref[g, ns:, :] = jnp.full((nsp - ns, QB), NEG, BF16)


def _attn_cmp(q_raw_t, kvc, kvct, proj, layout, nsp):
    s = q_raw_t.shape[2]
    G, QB, hd = NSA_KV_GROUPS, Q_BLOCK, NSA_HEAD_DIM
    nc1 = kvc.shape[2]
    ns = s // SEL_LEN
    ic = np.arange(nc1)[None, :]
    jc = np.arange(ns)[:, None]
    overlap_t = ((ic * CMP_STRIDE < (jc + 1) * SEL_LEN) & (ic * CMP_STRIDE + CMP_LEN > jc * SEL_LEN)
                 & (ic < nc1 - 1)).astype(np.float32)
    assert ns >= SEL_TOP
    chunk = _tile(nc1, 256)
    return pl.pallas_call(
        functools.partial(_attn_cmp_kernel, ns=ns, chunk=chunk),
        grid=(s // QB,),
        in_specs=[pl.BlockSpec((NSA_HEADS, hd, QB), lambda i: (0, 0, i)),
                  pl.BlockSpec((1, G, nc1, hd), lambda i: (0, 0, 0, 0)),
                  pl.BlockSpec((1, G, hd, nc1), lambda i: (1, 0, 0, 0)),
                  pl.BlockSpec((QB, G * LANES), lambda i: (i, _col_block(layout, "nsa", G * LANES))),
                  pl.BlockSpec((ns, nc1), lambda i: (0, 0))],
        out_specs=[pl.BlockSpec((QB, NSA_INNER), lambda i: (i, 0)),
                   pl.BlockSpec((G, nsp, QB), lambda i: (0, 0, i))],
        out_shape=[jax.ShapeDtypeStruct((s, NSA_INNER), F32), jax.ShapeDtypeStruct((G, nsp, s), BF16)],
        scratch_shapes=[pltpu.VMEM((ns, G * QB), F32)],
        compiler_params=_params(("parallel",)),
        name="attn_cmp",
    )(q_raw_t, kvc, kvct, proj, jnp.asarray(overlap_t, BF16))


def _attn_sel_kernel(qt_ref, sb_ref, ksa_ref, vt_ref, *rest, tk, slots, nwin, cb, look, unroll):
    kw_refs, vwt_refs = rest[:nwin], rest[nwin:2 * nwin]
    gt_ref, oc_ref, zb_ref, o_ref, qa_ref, m_ref, acc_ref = rest[2 * nwin:]
    QB, R, hd = Q_BLOCK, NSA_R, NSA_HEAD_DIM
    RQ = R * QB
    t0 = pl.program_id(1) * QB
    for hf in range(qa_ref.shape[0]):
        for r in range(R):
            qa_ref[hf, :hd, r * QB:(r + 1) * QB] = qt_ref[r]
            qa_ref[hf, hd:, r * QB:(r + 1) * QB] = sb_ref[0, hf * slots:(hf + 1) * slots, :]
    m_ref[...] = jnp.full(m_ref.shape, NEG, F32)
    acc_ref[...] = jnp.zeros(acc_ref.shape, F32)
    tiles_per_half = slots * SEL_LEN // tk

    ncb = RQ // cb

    def scores(kt, c):
        k0 = pl.multiple_of(kt * tk, tk)
        qa = qa_ref[kt // tiles_per_half, :, c * cb:(c + 1) * cb]
        return _dot(ksa_ref[0, pl.ds(k0, tk), :], qa)

    def accumulate(kt, c, s):
        cols = slice(c * cb, (c + 1) * cb)
        m_old = m_ref[:, cols]
        m_new = jnp.maximum(m_old, jnp.max(s, axis=0, keepdims=True))
        alpha = jnp.exp2(m_old - m_new)
        p = jnp.exp2(s - m_new).astype(BF16)
        m_ref[:, cols] = m_new
        acc_ref[:, cols] = alpha * acc_ref[:, cols] + _dot(vt_ref[0, kt], p)

    kt_last = (t0 + QB - 1) // tk

    def tiles(kt, pend, n):
        pend = list(pend)
        for t in range(n):
            for c in range(ncb):
                ahead = c + look
                pend.append(scores(kt + t, ahead) if ahead < ncb else scores(kt + t + 1, ahead - ncb))
                accumulate(kt + t, c, pend.pop(0))
        return tuple(pend)

    pend = tuple(scores(0, c) for c in range(look))
    done = 0
    for n in unroll:
        trips = (kt_last - done) // n
        pend = lax.fori_loop(0, trips, functools.partial(lambda u, p, n, done: tiles(done + u * n, p, n), n=n, done=done),
                             pend)
        done = done + trips * n
    pend = list(pend)
    qpos = t0 + (lax.broadcasted_iota(jnp.int32, (tk, cb), 1) & (QB - 1))
    causal = kt_last * tk + lax.broadcasted_iota(jnp.int32, (tk, cb), 0) <= qpos
    for c in range(ncb):
        if c + look < ncb:
            pend.append(scores(kt_last, c + look))
        accumulate(kt_last, c, jnp.where(causal, pend.pop(0), NEG))

    kw = jnp.concatenate([r[0] for r in kw_refs], axis=0)
    vwt = jnp.concatenate([r[0] for r in vwt_refs], axis=1)
    nk = nwin * QB
    kpos = t0 - WINDOW + lax.broadcasted_iota(jnp.int32, (nk, cb), 0)
    dlt = t0 + (lax.broadcasted_iota(jnp.int32, (nk, cb), 1) & (QB - 1)) - kpos
    vis = (dlt >= 0) & (dlt < WINDOW) & (kpos >= 0)
    gate = jax.nn.sigmoid(gt_ref[...])
    wscores = lambda c: _dot(kw, qa_ref[0, :hd, c * cb:(c + 1) * cb])
    pend = [wscores(c) for c in range(min(look, ncb))]
    for c in range(ncb):
        if c + look < ncb:
            pend.append(wscores(c + look))
        sw = jnp.where(vis, pend.pop(0), NEG)
        ew = jnp.exp2(sw - jnp.max(sw, axis=0, keepdims=True))
        owt = _dot(vwt, ew.astype(BF16)) / jnp.maximum(jnp.sum(ew, axis=0, keepdims=True), 1e-30)
        ost = acc_ref[:hd, c * cb:(c + 1) * cb] / jnp.maximum(acc_ref[hd:hd + 1, c * cb:(c + 1) * cb], 1e-30)
        for j in range(cb // QB):
            r = c * (cb // QB) + j
            cols = slice(r * hd, (r + 1) * hd)
            lanes = slice(j * QB, (j + 1) * QB)
            o = (oc_ref[:, cols] + ost[:, lanes].T * gate[:, R + r:R + r + 1]
                 + owt[:, lanes].T * gate[:, 2 * R + r:2 * R + r + 1])
            o_ref[:, cols] = (o * _silu(zb_ref[:, cols])).astype(o_ref.dtype)


def _attn_sel(q_rot_t, selb, ksa, vt, kw, vwt, proj, layout, oc, slots, tk):
    G, QB, hd, R = NSA_KV_GROUPS, Q_BLOCK, NSA_HEAD_DIM, NSA_R
    s = ksa.shape[1]
    nsp = selb.shape[1]
    nwin = WINDOW // QB + 1
    wblk =lambda j: (lambda g, i: (g, jnp.maximum(i - (nwin - 1) + j, 0), 0))
    wblk_t = lambda j: (lambda g, i: (g, 0, jnp.maximum(i - (nwin - 1) + j, 0)))
    oblk = pl.BlockSpec((QB, R * hd), lambda g, i: (i, g))
    return pl.pallas_call(
        functools.partial(_attn_sel_kernel, tk=tk, slots=slots, nwin=nwin, cb=SEL_COL_BLOCK, look=SEL_LOOKAHEAD,
                          unroll=SEL_UNROLL),
        grid=(G, s // QB),
        in_specs=[pl.BlockSpec((R, hd, QB), lambda g, i: (g, 0, i)),
                  pl.BlockSpec((1, nsp, QB), lambda g, i: (g, 0, i)),
                  pl.BlockSpec((1, s, hd + slots), lambda g, i: (g, 0, 0)),
                  pl.BlockSpec((1, s // tk, SEL_V_ROWS, tk), lambda g, i: (g, 0, 0, 0)),
                  *[pl.BlockSpec((1, QB, hd), wblk(j)) for j in range(nwin)],
                  *[pl.BlockSpec((1, hd, QB), wblk_t(j)) for j in range(nwin)],
                  pl.BlockSpec((QB, LANES), lambda g, i: (i, _col_block(layout, "nsa", LANES) + g)),
                  oblk,
                  pl.BlockSpec((QB, R * hd), lambda g, i: (i, _col_block(layout, "z_b", R * hd) + g))],
        out_specs=oblk,
        out_shape=jax.ShapeDtypeStruct((s, NSA_INNER), BF16),
        scratch_shapes=[pltpu.VMEM((nsp // slots, hd + slots, R * QB), BF16),
                        pltpu.VMEM((1, R * QB), F32), pltpu.VMEM((SEL_V_ROWS, R * QB), F32)],
        compiler_params=_params(("parallel", "arbitrary")),
        name="attn_sel",
    )(q_rot_t, selb, ksa, vt, *([kw] * nwin), *([vwt] * nwin), proj, oc, proj)


def _merge_kernel(ya_ref, ob_ref, wa_ref, wb_ref, ga_ref, gb_ref, m_ref):
    u_a = _dot(ya_ref[...], wa_ref[...])
    u_b = _dot(ob_ref[...], wb_ref[...])
    m_ref[...] = (jax.nn.sigmoid(ga_ref[...]) * u_a + jax.nn.sigmoid(gb_ref[...]) * u_b).astype(m_ref.dtype)


def _merge(y_a, ob, proj, layout, wa, wb):
    s = y_a.shape[0]
    d = wa.shape[1]
    tm, tn = _tile(s, 1024), _tile(d, 512)
    return pl.pallas_call(
        _merge_kernel,
        grid=(s // tm, d // tn),
        in_specs=[pl.BlockSpec((tm, SSM_INNER), lambda i, j: (i, 0)),
                  pl.BlockSpec((tm, NSA_INNER), lambda i, j: (i, 0)),
                  pl.BlockSpec((SSM_INNER, tn), lambda i, j: (0, j)),
                  pl.BlockSpec((NSA_INNER, tn), lambda i, j: (0, j)),
                  pl.BlockSpec((tm, tn), lambda i, j: (i, _col_block(layout, "g_a", tn) + j)),
                  pl.BlockSpec((tm, tn), lambda i, j: (i, _col_block(layout, "g_b", tn) + j))],
        out_specs=pl.BlockSpec((tm, tn), lambda i, j: (i, j)),
        out_shape=jax.ShapeDtypeStruct((s, d), BF16),
        compiler_params=_params(("parallel", "arbitrary")),
        name="merge",
    )(y_a, ob, wa, wb, proj, proj)


def _out_kernel(m_ref, w_ref, x_ref, gate_ref, fnw_ref, o_ref, r_ref, *, nt):
    j = pl.program_id(1)
    r_ref[j] = x_ref[...] + gate_ref[...] * _dot(m_ref[...], w_ref[...])

    @pl.when(j == nt - 1)
    def _():
        tn = r_ref.shape[2]
        ss = sum(jnp.sum(r_ref[t] * r_ref[t], axis=-1, keepdims=True) for t in range(nt))
        inv = lax.rsqrt(ss / (nt * tn) + EPS)
        for t in range(nt):
            o_ref[:, t * tn:(t + 1) * tn] = r_ref[t] * inv * fnw_ref[:, t * tn:(t + 1) * tn]


def _out_proj(merged, w_out, x2, mod, fnw):
    s, d = x2.shape
    tm, tn = min(512, s), min(512, d)
    nt = d // tn
    return pl.pallas_call(
        functools.partial(_out_kernel, nt=nt),
        grid=(s // tm, nt),
        in_specs=[pl.BlockSpec((tm, d), lambda i, j: (i, 0)),
                  pl.BlockSpec((d, tn), lambda i, j: (0, j)),
                  pl.BlockSpec((tm, tn), lambda i, j: (i, j)),
                  pl.BlockSpec((1, tn), lambda i, j: (0, 2 * nt + j)),
                  pl.BlockSpec((1, d), lambda i, j: (0, 0))],
        out_specs=pl.BlockSpec((tm, d), lambda i, j: (i, 0)),
        out_shape=jax.ShapeDtypeStruct((s, d), F32),
        scratch_shapes=[pltpu.VMEM((nt, tm, tn), F32)],
        compiler_params=_params(("parallel", "arbitrary")),
        name="out_proj",
    )(merged, w_out, x2, mod, fnw)


def _layer(x2, c, w_ada, b_ada, norm_w, w_in, conv_w, conv_b, dt_bias, a_log, d_skip, ssm_norm_w, cmp_pe_k, cmp_pe_v,
           cmp_k_w1, cmp_k_w2, cmp_v_w1, cmp_v_w2, w_proj_a, w_proj_b, w_out, final_norm_w, cos, sin):
    s, d = x2.shape
    G, hd = NSA_KV_GROUPS, NSA_HEAD_DIM
    layout, width = _proj_layout(d)
    n_pad = -(-width // 512) * 512
    slots = SEL_SLOTS
    nsp = max(s // SEL_LEN, slots)
    assert nsp % slots == 0 and (slots * SEL_LEN) % SEL_KEY_TILE == 0 and s % SEL_KEY_TILE == 0

    mod = _ada_mod(c.reshape(d, 1), w_ada, b_ada[None, :])
    proj = _in_proj(_norm_mod(x2, norm_w[None, :], mod), _build_w_in(w_in, d, n_pad))

    y_a = _ssd(proj, layout, conv_w, conv_b, dt_bias, a_log, d_skip, ssm_norm_w)

    q_raw_t, q_rot_t, ksa, vt, kw, vwt = _nsa_prep(proj, layout, cos, sin, slots, SEL_KEY_TILE)
    kv0 = layout["kv"]
    half_blocks = lambda t: jnp.transpose(t.reshape(s // CMP_STRIDE, CMP_STRIDE, G, hd), (2, 0, 1, 3)).reshape(
        G, s // CMP_STRIDE, CMP_STRIDE * hd)
    xkv = jnp.stack([half_blocks(proj[:, kv0:kv0 + KV_W]), half_blocks(proj[:, kv0 + KV_W:kv0 + 2 * KV_W])])
    pe = jnp.stack([cmp_pe_k.reshape(2, CMP_STRIDE * hd), cmp_pe_v.reshape(2, CMP_STRIDE * hd)])
    kvc, kvct = _nsa_cmp(xkv, pe, jnp.stack([cmp_k_w1, cmp_v_w1]).astype(BF16),
                         jnp.stack([cmp_k_w2, cmp_v_w2]).astype(BF16))

    oc, selb = _attn_cmp(q_raw_t, kvc, kvct, proj, layout, nsp)
    o = _attn_sel(q_rot_t, selb, ksa, vt, kw, vwt, proj, layout, oc, slots, SEL_KEY_TILE)

    merged = _merge(y_a, o, proj, layout, w_proj_a.astype(BF16), w_proj_b.astype(BF16))
    return _out_proj(merged, w_out.astype(BF16), x2, mod, final_norm_w[None, :])


def kernel(x, c, w_ada, b_ada, norm_w, w_in, conv_w, conv_b, dt_bias, a_log, d_skip, ssm_norm_w, cmp_pe_k, cmp_pe_v,
           cmp_k_w1, cmp_k_w2, cmp_v_w1, cmp_v_w2, w_proj_a, w_proj_b, w_out, final_norm_w):
    b, s, d = x.shape
    assert b == 1 and w_ada.shape[0] == 1, "one sequence, one layer"
    pos = jnp.arange(s, dtype=F32)
    inv_freq = ROPE_THETA ** (-jnp.arange(0, ROPE_DIM, 2, dtype=F32) / ROPE_DIM)
    ang = pos[:, None] * inv_freq[None, :]
    out = _layer(x[0], c, w_ada[0], b_ada[0], norm_w[0], w_in[0], conv_w[0], conv_b[0], dt_bias[0], a_log[0],
                 d_skip[0], ssm_norm_w[0], cmp_pe_k[0], cmp_pe_v[0], cmp_k_w1[0], cmp_k_w2[0], cmp_v_w1[0],
                 cmp_v_w2[0], w_proj_a[0], w_proj_b[0], w_out[0], final_norm_w, jnp.cos(ang), jnp.sin(ang))
    return out[None]
```

```python
import functools

import numpy as np
import jax
import jax.numpy as jnp
from jax import lax
from jax.experimental import pallas as pl
from jax.experimental.pallas import tpu as pltpu

F32 = jnp.float32
BF16 = jnp.bfloat16

EPS = 1e-6
BIG = 1e30
NEG = -1e30
REMOVED = -3e38

SSM_HEADS = 32
SSM_HEAD_DIM = 64
SSM_INNER = SSM_HEADS * SSM_HEAD_DIM
SSM_GROUPS = 8
SSM_STATE = 128
SSM_CONV = 4
SSM_CHUNK = 128
SSM_GN = SSM_GROUPS * SSM_STATE
SSM_XBC = SSM_INNER + 2 * SSM_GN
SSM_RP = SSM_INNER // SSM_GROUPS

NSA_HEADS = 16
NSA_KV_GROUPS = 2
NSA_R = NSA_HEADS // NSA_KV_GROUPS
NSA_HEAD_DIM = 128
NSA_INNER = NSA_HEADS * NSA_HEAD_DIM
KV_W = NSA_KV_GROUPS * NSA_HEAD_DIM
CMP_STRIDE = 16
CMP_LEN = 32
CMP_HIDDEN = 256
SEL_LEN = 64
SEL_TOP = 16
WINDOW = 512
Q_BLOCK = 128
ROPE_THETA = 500000.0
ROPE_DIM = NSA_HEAD_DIM // 4
ROPE_HALF = ROPE_DIM // 2

LANES = 128
SEL_SLOTS = 128
SEL_KEY_TILE = 512
SEL_COL_BLOCK = 256
SEL_LOOKAHEAD = 3
SEL_UNROLL = (8, 2, 1)
SEL_V_ROWS = NSA_HEAD_DIM + 16
LOG2E = 1.4426950408889634
VMEM_LIMIT = 56 * 1024 * 1024


def _params(sem, vmem=VMEM_LIMIT):
    return pltpu.CompilerParams(dimension_semantics=sem, vmem_limit_bytes=vmem)


def _tile(n, pref=512):
    t = min(pref, n)
    while n % t:
        t //= 2
    return t


def _col_block(layout, name, width):
    assert layout[name] % width == 0, (name, layout[name], width)
    return layout[name] // width


def _silu(v):
    return v * jax.nn.sigmoid(v)


def _softplus(v):
    return jnp.maximum(v, 0.0) + jnp.log1p(jnp.exp(-jnp.abs(v)))


def _split3(v):
    hi = v.astype(BF16)
    r = v - hi.astype(F32)
    mid = r.astype(BF16)
    lo = (r - mid.astype(F32)).astype(BF16)
    return hi, mid, lo


def _dot(a, b):
    return jnp.dot(a, b, preferred_element_type=F32)


def _dot_nt(a, b):
    return lax.dot_general(a, b, (((1,), (1,)), ((), ())), preferred_element_type=F32)


def _dot3_left(m01, v):
    hi, mid, lo = _split3(v)
    return _dot(m01, hi) + _dot(m01, mid) + _dot(m01, lo)


def _dot3_right(v, m01):
    hi, mid, lo = _split3(v)
    return _dot(hi, m01) + _dot(mid, m01) + _dot(lo, m01)


def _proj_layout(d_model):
    segs = [("xbc", SSM_XBC), ("g_a", d_model), ("g_b", d_model), ("z_a", SSM_INNER), ("q", NSA_INNER),
            ("z_b", NSA_INNER), ("kv", 6 * KV_W), ("nsa", NSA_KV_GROUPS * LANES), ("dt", LANES)]
    segs = sorted(segs, key=lambda s: -s[1])
    off, out = 0, {}
    for name, w in segs:
        out[name] = off
        off += w
    return out, off


def _src_offsets(d_model):
    splits = (SSM_INNER, SSM_XBC, SSM_HEADS, NSA_INNER, KV_W, KV_W, KV_W, KV_W, KV_W, KV_W,
              NSA_HEADS * 3, NSA_INNER, d_model, d_model)
    names = ("z_a", "xbc", "dt", "q", "kc", "vc", "ks", "vs", "kw", "vw", "nsa", "z_b", "g_a", "g_b")
    offs = np.concatenate([[0], np.cumsum(splits)])
    return {n: (int(offs[i]), int(splits[i])) for i, n in enumerate(names)}


def _build_w_in(w_in, d_model, n_pad):
    src = _src_offsets(d_model)
    layout, width = _proj_layout(d_model)

    def cols(name, w=None):
        a, n = src[name]
        return w_in[:, a:a + (n if w is None else w)]

    nsa = cols("nsa").reshape(d_model, NSA_KV_GROUPS, NSA_R, 3)
    nsa = jnp.transpose(nsa, (0, 1, 3, 2)).reshape(d_model, NSA_KV_GROUPS, 3 * NSA_R)
    nsa = jnp.pad(nsa, ((0, 0), (0, 0), (0, LANES - 3 * NSA_R))).reshape(d_model, NSA_KV_GROUPS * LANES)
    pieces = {
        "xbc": cols("xbc"), "g_a": cols("g_a"), "g_b": cols("g_b"), "z_a": cols("z_a"), "q": cols("q"),
        "z_b": cols("z_b"), "kv": cols("kc", 6 * KV_W), "nsa": nsa,
        "dt": jnp.pad(cols("dt"), ((0, 0), (0, LANES - SSM_HEADS))),
    }
    order = sorted(layout, key=lambda n: layout[n])
    parts = [pieces[n].astype(BF16) for n in order]
    if n_pad > width:
        parts.append(jnp.zeros((d_model, n_pad - width), BF16))
    return jnp.concatenate(parts, axis=1)


def _mod_kernel(c_ref, w_ref, b_ref, o_ref):
    sc = _silu(c_ref[...])
    o_ref[...] = jnp.sum(w_ref[...] * sc, axis=0, keepdims=True) + b_ref[...]


def _ada_mod(c_col, w_ada, b_ada):
    d, n = w_ada.shape
    tn = _tile(n)
    return pl.pallas_call(
        _mod_kernel,
        grid=(n // tn,),
        in_specs=[pl.BlockSpec((d, 1), lambda j: (0, 0)),
                  pl.BlockSpec((d, tn), lambda j: (0, j)),
                  pl.BlockSpec((1, tn), lambda j: (0, j))],
        out_specs=pl.BlockSpec((1, tn), lambda j: (0, j)),
        out_shape=jax.ShapeDtypeStruct((1, n), F32),
        compiler_params=_params(("arbitrary",)),
        name="ada_mod",
    )(c_col, w_ada, b_ada)


def _norm_mod_kernel(x_ref, nw_ref, shift_ref, scl_ref, h_ref):
    x = x_ref[...]
    y = x * lax.rsqrt(jnp.mean(x * x, axis=-1, keepdims=True) + EPS) * nw_ref[...]
    h_ref[...] = (y * (1.0 + scl_ref[...]) + shift_ref[...]).astype(BF16)


def _norm_mod(x2, norm_w, mod):
    s, d = x2.shape
    tm = _tile(s, 256)
    return pl.pallas_call(
        _norm_mod_kernel,
        grid=(s // tm,),
        in_specs=[pl.BlockSpec((tm, d), lambda i: (i, 0)),
                  pl.BlockSpec((1, d), lambda i: (0, 0)),
                  pl.BlockSpec((1, d), lambda i: (0, 0)),
                  pl.BlockSpec((1, d), lambda i: (0, 1))],
        out_specs=pl.BlockSpec((tm, d), lambda i: (i, 0)),
        out_shape=jax.ShapeDtypeStruct((s, d), BF16),
        compiler_params=_params(("parallel",)),
        name="norm_mod",
    )(x2, norm_w, mod, mod)


def _in_proj_kernel(h_ref, w_ref, o_ref):
    o_ref[...] = _dot(h_ref[...], w_ref[...])


def _in_proj(h, w_in_p):
    s, d = h.shape
    n = w_in_p.shape[1]
    tm, tn = _tile(s, 1024), _tile(n, 1024)
    return pl.pallas_call(
        _in_proj_kernel,
        grid=(s // tm, n // tn),
        in_specs=[pl.BlockSpec((tm, d), lambda i, j: (i, 0)),
                  pl.BlockSpec((d, tn), lambda i, j: (0, j))],
        out_specs=pl.BlockSpec((tm, tn), lambda i, j: (i, j)),
        out_shape=jax.ShapeDtypeStruct((s, n), F32),
        compiler_params=_params(("parallel", "arbitrary")),
        name="in_proj",
    )(h, w_in_p)


def _ssd_kernel(xbc_ref, za_ref, dt_ref, cw_ref, cb_ref, dsk_ref, nw_ref, dtbs_ref, as_ref,
                e_ref, o_ref, xp_ref, st_ref, y_ref):
    L, N, P, RP = SSM_CHUNK, SSM_STATE, SSM_HEAD_DIM, SSM_RP

    @pl.when(pl.program_id(0) == 0)
    def _():
        st_ref[...] = jnp.zeros_like(st_ref)
        xp_ref[pl.ds(L, 8), :] = jnp.zeros((8, SSM_XBC), F32)

    xp_ref[pl.ds(0, 8), :] = xp_ref[pl.ds(L, 8), :]
    x = xbc_ref[...]
    xp_ref[pl.ds(8, L), :] = x
    cw = cw_ref[...]
    conv = (cb_ref[...] + cw[3:4] * x + cw[2:3] * xp_ref[pl.ds(7, L), :]
            + cw[1:2] * xp_ref[pl.ds(6, L), :] + cw[0:1] * xp_ref[pl.ds(5, L), :])
    act = _silu(conv)
    xs = act[:, :SSM_INNER]
    bm = act[:, SSM_INNER:SSM_INNER + SSM_GN]
    cm = act[:, SSM_INNER + SSM_GN:]

    row = lax.broadcasted_iota(jnp.int32, (L, L), 0)
    col = lax.broadcasted_iota(jnp.int32, (L, L), 1)
    causal = row >= col
    tril = causal.astype(BF16)

    dt_s = _softplus(dt_ref[...] + dtbs_ref[...])
    a_cs_s = _dot3_left(tril, dt_s * as_ref[...])
    a_cs_t = a_cs_s.T
    dt = _dot3_right(dt_s, e_ref[...])
    a_cs = _dot3_right(a_cs_s, e_ref[...])

    xdt = xs * dt
    ea = jnp.exp(a_cs)
    a_last = a_cs[L - 1:L, :]
    cdec = jnp.exp(a_last)
    xdt_b = xdt.astype(BF16)
    xw_b = (xdt * jnp.exp(a_last - a_cs)).astype(BF16)

    for g in range(SSM_GROUPS):
        bg = bm[:, g * N:(g + 1) * N]
        cg_b = cm[:, g * N:(g + 1) * N].astype(BF16)
        cb = _dot_nt(cg_b, bg.astype(BF16))
        st = st_ref[g]
        y_off = _dot(cg_b, st.astype(BF16)) * ea[:, g * RP:(g + 1) * RP]
        for r in range(SSM_HEADS // SSM_GROUPS):
            h = g * (SSM_HEADS // SSM_GROUPS) + r
            seg = a_cs_s[:, h:h + 1] - a_cs_t[h:h + 1, :]
            dec = jnp.exp(jnp.where(causal, seg, NEG))
            y_d = _dot((cb * dec).astype(BF16), xdt_b[:, h * P:(h + 1) * P])
            y_ref[:, h * P:(h + 1) * P] = y_d + y_off[:, r * P:(r + 1) * P]
        new = _dot(bg.T.astype(BF16), xw_b[:, g * RP:(g + 1) * RP])
        st_ref[g] = st * cdec[:, g * RP:(g + 1) * RP] + new

    y = y_ref[...] + dsk_ref[...] * xs
    yg = y * _silu(za_ref[...])
    o_ref[...] = (yg * lax.rsqrt(jnp.mean(yg * yg, axis=-1, keepdims=True) + EPS) * nw_ref[...]).astype(o_ref.dtype)


def _ssd(proj, layout, conv_w, conv_b, dt_bias, a_log, d_skip, ssm_norm_w):
    s = proj.shape[0]
    L = SSM_CHUNK
    rep = lambda v: jnp.repeat(v.astype(F32), SSM_HEAD_DIM)[None, :]
    pad = lambda v: jnp.pad(v.astype(F32), (0, LANES - SSM_HEADS))[None, :]
    a = -jnp.exp(a_log.astype(F32))
    expand = np.zeros((LANES, SSM_INNER), np.float32)
    expand[np.arange(SSM_INNER) // SSM_HEAD_DIM, np.arange(SSM_INNER)] = 1.0
    const = lambda shape: pl.BlockSpec(shape, lambda c: (0,) * len(shape))
    return pl.pallas_call(
        _ssd_kernel,
        grid=(s // L,),
        in_specs=[pl.BlockSpec((L, SSM_XBC), lambda c: (c, _col_block(layout, "xbc", SSM_XBC))),
                  pl.BlockSpec((L, SSM_INNER), lambda c: (c, _col_block(layout, "z_a", SSM_INNER))),
                  pl.BlockSpec((L, LANES), lambda c: (c, _col_block(layout, "dt", LANES))),
                  const((SSM_CONV, SSM_XBC)), const((1, SSM_XBC)),
                  const((1, SSM_INNER)), const((1, SSM_INNER)),
                  const((1, LANES)), const((1, LANES)), const((LANES, SSM_INNER))],
        out_specs=pl.BlockSpec((L, SSM_INNER), lambda c: (c, 0)),
        out_shape=jax.ShapeDtypeStruct((s, SSM_INNER), BF16),
        scratch_shapes=[pltpu.VMEM((L + 8, SSM_XBC), F32),
                        pltpu.VMEM((SSM_GROUPS, SSM_STATE, SSM_RP), F32),
                        pltpu.VMEM((L, SSM_INNER), F32)],
        compiler_params=_params(("arbitrary",)),
        name="ssd",
    )(proj, proj, proj, conv_w, conv_b[None, :], rep(d_skip), ssm_norm_w[None, :],
      pad(dt_bias), pad(a), jnp.asarray(expand, BF16))


def _prep_kernel(q_ref, ks_ref, vsi_ref, kwi_ref, vwi_ref, cos_ref, sa_ref, sb_ref, qr_ref, qt_ref, ksa_ref, vt_ref,
                 kw_ref, vwt_ref, *, tm, slots):
    cosf, sa, sb = cos_ref[...], sa_ref[...], sb_ref[...]
    hd = NSA_HEAD_DIM

    def rope(t):
        return t * cosf + pltpu.roll(t, hd - ROPE_HALF, 1) * sa + pltpu.roll(t, ROPE_HALF, 1) * sb

    scale = hd ** -0.5 * LOG2E
    for h in range(NSA_HEADS):
        t = q_ref[:, h * hd:(h + 1) * hd]
        qr_ref[h] = (t * scale).T.astype(BF16)
        qt_ref[h] = (rope(t) * scale).T.astype(BF16)

    pos = pl.program_id(0) * tm + lax.broadcasted_iota(jnp.int32, (tm, slots), 0)
    lane = lax.broadcasted_iota(jnp.int32, (tm, slots), 1)
    onehot = jnp.where((pos // SEL_LEN) % slots == lane, 1.0, 0.0).astype(BF16)
    for g in range(NSA_KV_GROUPS):
        cols = slice(g * hd, (g + 1) * hd)
        ksa_ref[g, :, :hd] = rope(ks_ref[:, cols]).astype(BF16)
        ksa_ref[g, :, hd:] = onehot
        vt_ref[g, 0, :hd, :] = vsi_ref[:, cols].T.astype(BF16)
        vt_ref[g, 0, hd:, :] = jnp.ones((SEL_V_ROWS - hd, tm), BF16)
        kw_ref[g] = rope(kwi_ref[:, cols]).astype(BF16)
        vwt_ref[g] = vwi_ref[:, cols].T.astype(BF16)


def _nsa_prep(proj, layout, cos, sin, slots, tk):
    s = proj.shape[0]
    tm = _tile(tk, 512)
    hd, G = NSA_HEAD_DIM, NSA_KV_GROUPS
    ones = jnp.ones((s, hd - ROPE_DIM), F32)
    zeros = lambda w: jnp.zeros((s, w), F32)
    cosf = jnp.concatenate([cos, cos, ones], axis=1)
    sa = jnp.concatenate([-sin, zeros(hd - ROPE_HALF)], axis=1)
    sb = jnp.concatenate([zeros(ROPE_HALF), sin, zeros(hd - ROPE_DIM)], axis=1)
    tab = pl.BlockSpec((tm, hd), lambda i: (i, 0))
    grp = lambda w: pl.BlockSpec((G, tm, w), lambda i: (0, i, 0))
    kvb = _col_block(layout, "kv", KV_W)
    return pl.pallas_call(
        functools.partial(_prep_kernel, tm=tm, slots=slots),
        grid=(s // tm,),
        in_specs=[pl.BlockSpec((tm, NSA_INNER), lambda i: (i, _col_block(layout, "q", NSA_INNER))),
                  *[pl.BlockSpec((tm, KV_W), functools.partial(lambda i, k: (i, k), k=kvb + k)) for k in (2, 3, 4, 5)],
                  tab, tab, tab],
        out_specs=[pl.BlockSpec((NSA_HEADS, hd, tm), lambda i: (0, 0, i)),
                   pl.BlockSpec((NSA_HEADS, hd, tm), lambda i: (0, 0, i)),
                   grp(hd + slots),
                   pl.BlockSpec((G, 1, SEL_V_ROWS, tm), lambda i: (0, i // (tk // tm), 0, i % (tk // tm))),
                   grp(hd),
                   pl.BlockSpec((G, hd, tm), lambda i: (0, 0, i))],
        out_shape=[jax.ShapeDtypeStruct((NSA_HEADS, hd, s), BF16), jax.ShapeDtypeStruct((NSA_HEADS, hd, s), BF16),
                   jax.ShapeDtypeStruct((G, s, hd + slots), BF16),
                   jax.ShapeDtypeStruct((G, s // tk, SEL_V_ROWS, tk), BF16),
                   jax.ShapeDtypeStruct((G, s, hd), BF16), jax.ShapeDtypeStruct((G, hd, s), BF16)],
        compiler_params=_params(("parallel",)),
        name="nsa_prep",
    )(proj, proj, proj, proj, proj, cosf, sa, sb)


def _cmp_kernel(x_ref, pe_ref, w1_ref, w2_ref, o_ref, ot_ref):
    x = x_ref[0, 0]
    nc1, half = x.shape
    a = _dot((x + pe_ref[0, 0:1]).astype(BF16), w1_ref[0, :half])
    b = _dot((x + pe_ref[0, 1:2]).astype(BF16), w1_ref[0, half:])
    pre = a + pltpu.roll(b, nc1 - 1, 0)
    hdn = 0.5 * pre * (1.0 + jnp.tanh(0.7978845608028654 * (pre + 0.044715 * pre * pre * pre)))
    out = _dot(hdn.astype(BF16), w2_ref[0])
    o_ref[0, 0] = out.astype(o_ref.dtype)
    ot_ref[0, 0] = out.T.astype(ot_ref.dtype)


def _nsa_cmp(xkv, pe, w1, w2):
    _, G, nc1, half = xkv.shape
    return pl.pallas_call(
        _cmp_kernel,
        grid=(2, G),
        in_specs=[pl.BlockSpec((1, 1, nc1, half), lambda t, g: (t, g, 0, 0)),
                  pl.BlockSpec((1, 2, half), lambda t, g: (t, 0, 0)),
                  pl.BlockSpec((1, 2 * half, CMP_HIDDEN), lambda t, g: (t, 0, 0)),
                  pl.BlockSpec((1, CMP_HIDDEN, NSA_HEAD_DIM), lambda t, g: (t, 0, 0))],
        out_specs=[pl.BlockSpec((1, 1, nc1, NSA_HEAD_DIM), lambda t, g: (t, g, 0, 0)),
                   pl.BlockSpec((1, 1, NSA_HEAD_DIM, nc1), lambda t, g: (t, g, 0, 0))],
        out_shape=[jax.ShapeDtypeStruct((2, G, nc1, NSA_HEAD_DIM), BF16),
                   jax.ShapeDtypeStruct((2, G, NSA_HEAD_DIM, nc1), BF16)],
        compiler_params=_params(("parallel", "parallel")),
        name="nsa_cmp",
    )(xkv, pe, w1, w2)


def _attn_cmp_kernel(qt_ref, kc_ref, vct_ref, gt_ref, ovt_ref, oc_ref, sb_ref, imp_ref, *, ns, chunk):
    QB, R, hd, G = Q_BLOCK, NSA_R, NSA_HEAD_DIM, NSA_KV_GROUPS
    i = pl.program_id(0)
    t0 = i * QB
    nc1 = kc_ref.shape[2]
    gate = jax.nn.sigmoid(gt_ref[...])

    def branch(nk):
        pos = t0 + (lax.broadcasted_iota(jnp.int32, (nk, R * QB), 1) & (QB - 1))
        kend = lax.broadcasted_iota(jnp.int32, (nk, R * QB), 0) * CMP_STRIDE + (CMP_LEN - 1)
        vis = kend <= pos
        for g in range(G):
            qt = jnp.concatenate([qt_ref[g * R + r] for r in range(R)], axis=1)
            sm = jnp.where(vis, _dot(kc_ref[0, g, :nk, :], qt), NEG)
            e = jnp.where(vis, jnp.exp2(sm - jnp.max(sm, axis=0, keepdims=True)), 0.0)
            p = e / jnp.maximum(jnp.sum(e, axis=0, keepdims=True), 1e-30)
            oct = _dot(vct_ref[0, g, :, :nk], p.astype(BF16))
            psum = p[:, :QB]
            for r in range(R):
                h = g * R + r
                oc_ref[:, h * hd:(h + 1) * hd] = oct[:, r * QB:(r + 1) * QB].T * gate[:, g * LANES + r:g * LANES + r + 1]
                if r:
                    psum = psum + p[:, r * QB:(r + 1) * QB]
            imp_ref[:, g * QB:(g + 1) * QB] = _dot3_left(ovt_ref[:, :nk], psum)

    need = (i * (QB // CMP_STRIDE) + (QB - CMP_LEN) // CMP_STRIDE + chunk) // chunk
    for n in range(1, nc1 // chunk + 1):
        pl.when(need == n)(functools.partial(branch, n * chunk))

    shape = (ns, G * QB)
    imp_t = imp_ref[...]
    jb = lax.broadcasted_iota(jnp.int32, shape, 0)
    pq = t0 + (lax.broadcasted_iota(jnp.int32, shape, 1) & (QB - 1))
    cur = pq // SEL_LEN
    eligible = jb * SEL_LEN <= pq
    forced = (jb == 0) | (jb == cur) | (jb == cur - 1)
    score = jnp.where(forced, REMOVED, jnp.where(eligible, imp_t, -BIG))
    jbf = jb.astype(F32)

    def take(_, score):
        mx = jnp.max(score, axis=0, keepdims=True)
        idx = jnp.min(jnp.where(score == mx, jbf, float(ns)), axis=0, keepdims=True)
        return jnp.where(jbf == idx, REMOVED, score)

    score = lax.fori_loop(0, SEL_TOP - 3, take, score)
    bias = jnp.where((score == REMOVED) & eligible, 0.0, NEG).astype(BF16)
    nsp = sb_ref.shape[1]
    for g in range(G):
        sb_ref[g, :ns, :] = bias[:, g * QB:(g + 1) * QB]
        if nsp > ns:
            sb_ref[g, ns:, :] = jnp.full((nsp - ns, QB), NEG, BF16)


def _attn_cmp(q_raw_t, kvc, kvct, proj, layout, nsp):
    s = q_raw_t.shape[2]
    G, QB, hd = NSA_KV_GROUPS, Q_BLOCK, NSA_HEAD_DIM
    nc1 = kvc.shape[2]
    ns = s // SEL_LEN
    ic = np.arange(nc1)[None, :]
    jc = np.arange(ns)[:, None]
    overlap_t = ((ic * CMP_STRIDE < (jc + 1) * SEL_LEN) & (ic * CMP_STRIDE + CMP_LEN > jc * SEL_LEN)
                 & (ic < nc1 - 1)).astype(np.float32)
    assert ns >= SEL_TOP
    chunk = _tile(nc1, 256)
    return pl.pallas_call(
        functools.partial(_attn_cmp_kernel, ns=ns, chunk=chunk),
        grid=(s // QB,),
        in_specs=[pl.BlockSpec((NSA_HEADS, hd, QB), lambda i: (0, 0, i)),
                  pl.BlockSpec((1, G, nc1, hd), lambda i: (0, 0, 0, 0)),
                  pl.BlockSpec((1, G, hd, nc1), lambda i: (1, 0, 0, 0)),
                  pl.BlockSpec((QB, G * LANES), lambda i: (i, _col_block(layout, "nsa", G * LANES))),
                  pl.BlockSpec((ns, nc1), lambda i: (0, 0))],
        out_specs=[pl.BlockSpec((QB, NSA_INNER), lambda i: (i, 0)),
                   pl.BlockSpec((G, nsp, QB), lambda i: (0, 0, i))],
        out_shape=[jax.ShapeDtypeStruct((s, NSA_INNER), F32), jax.ShapeDtypeStruct((G, nsp, s), BF16)],
        scratch_shapes=[pltpu.VMEM((ns, G * QB), F32)],
        compiler_params=_params(("parallel",)),
        name="attn_cmp",
    )(q_raw_t, kvc, kvct, proj, jnp.asarray(overlap_t, BF16))


def _attn_sel_kernel(qt_ref, sb_ref, ksa_ref, vt_ref, *rest, tk, slots, nwin, cb, look, unroll):
    kw_refs, vwt_refs = rest[:nwin], rest[nwin:2 * nwin]
    gt_ref, oc_ref, zb_ref, o_ref, qa_ref, m_ref, acc_ref = rest[2 * nwin:]
    QB, R, hd = Q_BLOCK, NSA_R, NSA_HEAD_DIM
    RQ = R * QB
    t0 = pl.program_id(1) * QB
    for hf in range(qa_ref.shape[0]):
        for r in range(R):
            qa_ref[hf, :hd, r * QB:(r + 1) * QB] = qt_ref[r]
            qa_ref[hf, hd:, r * QB:(r + 1) * QB] = sb_ref[0, hf * slots:(hf + 1) * slots, :]
    m_ref[...] = jnp.full(m_ref.shape, NEG, F32)
    acc_ref[...] = jnp.zeros(acc_ref.shape, F32)
    tiles_per_half = slots * SEL_LEN // tk

    ncb = RQ // cb

    def scores(kt, c):
        k0 = pl.multiple_of(kt * tk, tk)
        qa = qa_ref[kt // tiles_per_half, :, c * cb:(c + 1) * cb]
        return _dot(ksa_ref[0, pl.ds(k0, tk), :], qa)

    def accumulate(kt, c, s):
        cols = slice(c * cb, (c + 1) * cb)
        m_old = m_ref[:, cols]
        m_new = jnp.maximum(m_old, jnp.max(s, axis=0, keepdims=True))
        alpha = jnp.exp2(m_old - m_new)
        p = jnp.exp2(s - m_new).astype(BF16)
        m_ref[:, cols] = m_new
        acc_ref[:, cols] = alpha * acc_ref[:, cols] + _dot(vt_ref[0, kt], p)

    kt_last = (t0 + QB - 1) // tk

    def tiles(kt, pend, n):
        pend = list(pend)
        for t in range(n):
            for c in range(ncb):
                ahead = c + look
                pend.append(scores(kt + t, ahead) if ahead < ncb else scores(kt + t + 1, ahead - ncb))
                accumulate(kt + t, c, pend.pop(0))
        return tuple(pend)

    pend = tuple(scores(0, c) for c in range(look))
    done = 0
    for n in unroll:
        trips = (kt_last - done) // n
        pend = lax.fori_loop(0, trips, functools.partial(lambda u, p, n, done: tiles(done + u * n, p, n), n=n, done=done),
                             pend)
        done = done + trips * n
    pend = list(pend)
    qpos = t0 + (lax.broadcasted_iota(jnp.int32, (tk, cb), 1) & (QB - 1))
    causal = kt_last * tk + lax.broadcasted_iota(jnp.int32, (tk, cb), 0) <= qpos
    for c in range(ncb):
        if c + look < ncb:
            pend.append(scores(kt_last, c + look))
        accumulate(kt_last, c, jnp.where(causal, pend.pop(0), NEG))

    kw = jnp.concatenate([r[0] for r in kw_refs], axis=0)
    vwt = jnp.concatenate([r[0] for r in vwt_refs], axis=1)
    nk = nwin * QB
    kpos = t0 - WINDOW + lax.broadcasted_iota(jnp.int32, (nk, cb), 0)
    dlt = t0 + (lax.broadcasted_iota(jnp.int32, (nk, cb), 1) & (QB - 1)) - kpos
    vis = (dlt >= 0) & (dlt < WINDOW) & (kpos >= 0)
    gate = jax.nn.sigmoid(gt_ref[...])
    wscores = lambda c: _dot(kw, qa_ref[0, :hd, c * cb:(c + 1) * cb])
    pend = [wscores(c) for c in range(min(look, ncb))]
    for c in range(ncb):
        if c + look < ncb:
            pend.append(wscores(c + look))
        sw = jnp.where(vis, pend.pop(0), NEG)
        ew = jnp.exp2(sw - jnp.max(sw, axis=0, keepdims=True))
        owt = _dot(vwt, ew.astype(BF16)) / jnp.maximum(jnp.sum(ew, axis=0, keepdims=True), 1e-30)
        ost = acc_ref[:hd, c * cb:(c + 1) * cb] / jnp.maximum(acc_ref[hd:hd + 1, c * cb:(c + 1) * cb], 1e-30)
        for j in range(cb // QB):
            r = c * (cb // QB) + j
            cols = slice(r * hd, (r + 1) * hd)
            lanes = slice(j * QB, (j + 1) * QB)
            o = (oc_ref[:, cols] + ost[:, lanes].T * gate[:, R + r:R + r + 1]
                 + owt[:, lanes].T * gate[:, 2 * R + r:2 * R + r + 1])
            o_ref[:, cols] = (o * _silu(zb_ref[:, cols])).astype(o_ref.dtype)


def _attn_sel(q_rot_t, selb, ksa, vt, kw, vwt, proj, layout, oc, slots, tk):
    G, QB, hd, R = NSA_KV_GROUPS, Q_BLOCK, NSA_HEAD_DIM, NSA_R
    s = ksa.shape[1]
    nsp = selb.shape[1]
    nwin = WINDOW // QB + 1
    wblk =lambda j: (lambda g, i: (g, jnp.maximum(i - (nwin - 1) + j, 0), 0))
    wblk_t = lambda j: (lambda g, i: (g, 0, jnp.maximum(i - (nwin - 1) + j, 0)))
    oblk = pl.BlockSpec((QB, R * hd), lambda g, i: (i, g))
    return pl.pallas_call(
        functools.partial(_attn_sel_kernel, tk=tk, slots=slots, nwin=nwin, cb=SEL_COL_BLOCK, look=SEL_LOOKAHEAD,
                          unroll=SEL_UNROLL),
        grid=(G, s // QB),
        in_specs=[pl.BlockSpec((R, hd, QB), lambda g, i: (g, 0, i)),
                  pl.BlockSpec((1, nsp, QB), lambda g, i: (g, 0, i)),
                  pl.BlockSpec((1, s, hd + slots), lambda g, i: (g, 0, 0)),
                  pl.BlockSpec((1, s // tk, SEL_V_ROWS, tk), lambda g, i: (g, 0, 0, 0)),
                  *[pl.BlockSpec((1, QB, hd), wblk(j)) for j in range(nwin)],
                  *[pl.BlockSpec((1, hd, QB), wblk_t(j)) for j in range(nwin)],
                  pl.BlockSpec((QB, LANES), lambda g, i: (i, _col_block(layout, "nsa", LANES) + g)),
                  oblk,
                  pl.BlockSpec((QB, R * hd), lambda g, i: (i, _col_block(layout, "z_b", R * hd) + g))],
        out_specs=oblk,
        out_shape=jax.ShapeDtypeStruct((s, NSA_INNER), BF16),
        scratch_shapes=[pltpu.VMEM((nsp // slots, hd + slots, R * QB), BF16),
                        pltpu.VMEM((1, R * QB), F32), pltpu.VMEM((SEL_V_ROWS, R * QB), F32)],
        compiler_params=_params(("parallel", "arbitrary")),
        name="attn_sel",
    )(q_rot_t, selb, ksa, vt, *([kw] * nwin), *([vwt] * nwin), proj, oc, proj)


def _merge_kernel(ya_ref, ob_ref, wa_ref, wb_ref, ga_ref, gb_ref, m_ref):
    u_a = _dot(ya_ref[...], wa_ref[...])
    u_b = _dot(ob_ref[...], wb_ref[...])
    m_ref[...] = (jax.nn.sigmoid(ga_ref[...]) * u_a + jax.nn.sigmoid(gb_ref[...]) * u_b).astype(m_ref.dtype)


def _merge(y_a, ob, proj, layout, wa, wb):
    s = y_a.shape[0]
    d = wa.shape[1]
    tm, tn = _tile(s, 1024), _tile(d, 512)
    return pl.pallas_call(
        _merge_kernel,
        grid=(s // tm, d // tn),
        in_specs=[pl.BlockSpec((tm, SSM_INNER), lambda i, j: (i, 0)),
                  pl.BlockSpec((tm, NSA_INNER), lambda i, j: (i, 0)),
                  pl.BlockSpec((SSM_INNER, tn), lambda i, j: (0, j)),
                  pl.BlockSpec((NSA_INNER, tn), lambda i, j: (0, j)),
                  pl.BlockSpec((tm, tn), lambda i, j: (i, _col_block(layout, "g_a", tn) + j)),
                  pl.BlockSpec((tm, tn), lambda i, j: (i, _col_block(layout, "g_b", tn) + j))],
        out_specs=pl.BlockSpec((tm, tn), lambda i, j: (i, j)),
        out_shape=jax.ShapeDtypeStruct((s, d), BF16),
        compiler_params=_params(("parallel", "arbitrary")),
        name="merge",
    )(y_a, ob, wa, wb, proj, proj)


def _out_kernel(m_ref, w_ref, x_ref, gate_ref, fnw_ref, o_ref, r_ref, *, nt):
    j = pl.program_id(1)
    r_ref[j] = x_ref[...] + gate_ref[...] * _dot(m_ref[...], w_ref[...])

    @pl.when(j == nt - 1)
    def _():
        tn = r_ref.shape[2]
        ss = sum(jnp.sum(r_ref[t] * r_ref[t], axis=-1, keepdims=True) for t in range(nt))
        inv = lax.rsqrt(ss / (nt * tn) + EPS)
        for t in range(nt):
            o_ref[:, t * tn:(t + 1) * tn] = r_ref[t] * inv * fnw_ref[:, t * tn:(t + 1) * tn]


def _out_proj(merged, w_out, x2, mod, fnw):
    s, d = x2.shape
    tm, tn = min(512, s), min(512, d)
    nt = d // tn
    return pl.pallas_call(
        functools.partial(_out_kernel, nt=nt),
        grid=(s // tm, nt),
        in_specs=[pl.BlockSpec((tm, d), lambda i, j: (i, 0)),
                  pl.BlockSpec((d, tn), lambda i, j: (0, j)),
                  pl.BlockSpec((tm, tn), lambda i, j: (i, j)),
                  pl.BlockSpec((1, tn), lambda i, j: (0, 2 * nt + j)),
                  pl.BlockSpec((1, d), lambda i, j: (0, 0))],
        out_specs=pl.BlockSpec((tm, d), lambda i, j: (i, 0)),
        out_shape=jax.ShapeDtypeStruct((s, d), F32),
        scratch_shapes=[pltpu.VMEM((nt, tm, tn), F32)],
        compiler_params=_params(("parallel", "arbitrary")),
        name="out_proj",
    )(merged, w_out, x2, mod, fnw)


def _layer(x2, c, w_ada, b_ada, norm_w, w_in, conv_w, conv_b, dt_bias, a_log, d_skip, ssm_norm_w, cmp_pe_k, cmp_pe_v,
           cmp_k_w1, cmp_k_w2, cmp_v_w1, cmp_v_w2, w_proj_a, w_proj_b, w_out, final_norm_w, cos, sin):
    s, d = x2.shape
    G, hd = NSA_KV_GROUPS, NSA_HEAD_DIM
    layout, width = _proj_layout(d)
    n_pad = -(-width // 512) * 512
    slots = SEL_SLOTS
    nsp = max(s // SEL_LEN, slots)
    assert nsp % slots == 0 and (slots * SEL_LEN) % SEL_KEY_TILE == 0 and s % SEL_KEY_TILE == 0

    mod = _ada_mod(c.reshape(d, 1), w_ada, b_ada[None, :])
    proj = _in_proj(_norm_mod(x2, norm_w[None, :], mod), _build_w_in(w_in, d, n_pad))

    y_a = _ssd(proj, layout, conv_w, conv_b, dt_bias, a_log, d_skip, ssm_norm_w)

    q_raw_t, q_rot_t, ksa, vt, kw, vwt = _nsa_prep(proj, layout, cos, sin, slots, SEL_KEY_TILE)
    kv0 = layout["kv"]
    half_blocks = lambda t: jnp.transpose(t.reshape(s // CMP_STRIDE, CMP_STRIDE, G, hd), (2, 0, 1, 3)).reshape(
        G, s // CMP_STRIDE, CMP_STRIDE * hd)
    xkv = jnp.stack([half_blocks(proj[:, kv0:kv0 + KV_W]), half_blocks(proj[:, kv0 + KV_W:kv0 + 2 * KV_W])])
    pe = jnp.stack([cmp_pe_k.reshape(2, CMP_STRIDE * hd), cmp_pe_v.reshape(2, CMP_STRIDE * hd)])
    kvc, kvct = _nsa_cmp(xkv, pe, jnp.stack([cmp_k_w1, cmp_v_w1]).astype(BF16),
                         jnp.stack([cmp_k_w2, cmp_v_w2]).astype(BF16))

    oc, selb = _attn_cmp(q_raw_t, kvc, kvct, proj, layout, nsp)
    o = _attn_sel(q_rot_t, selb, ksa, vt, kw, vwt, proj, layout, oc, slots, SEL_KEY_TILE)

    merged = _merge(y_a, o, proj, layout, w_proj_a.astype(BF16), w_proj_b.astype(BF16))
    return _out_proj(merged, w_out.astype(BF16), x2, mod, final_norm_w[None, :])


def kernel(x, c, w_ada, b_ada, norm_w, w_in, conv_w, conv_b, dt_bias, a_log, d_skip, ssm_norm_w, cmp_pe_k, cmp_pe_v,
           cmp_k_w1, cmp_k_w2, cmp_v_w1, cmp_v_w2, w_proj_a, w_proj_b, w_out, final_norm_w):
    b, s, d = x.shape
    assert b == 1 and w_ada.shape[0] == 1, "one sequence, one layer"
    pos = jnp.arange(s, dtype=F32)
    inv_freq = ROPE_THETA ** (-jnp.arange(0, ROPE_DIM, 2, dtype=F32) / ROPE_DIM)
    ang = pos[:, None] * inv_freq[None, :]
    out = _layer(x[0], c, w_ada[0], b_ada[0], norm_w[0], w_in[0], conv_w[0], conv_b[0], dt_bias[0], a_log[0],
                 d_skip[0], ssm_norm_w[0], cmp_pe_k[0], cmp_pe_v[0], cmp_k_w1[0], cmp_k_w2[0], cmp_v_w1[0],
                 cmp_v_w2[0], w_proj_a[0], w_proj_b[0], w_out[0], final_norm_w, jnp.cos(ang), jnp.sin(ang))
    return out[None]
```
